```python
import math
import jax, jax.numpy as jnp
from jax import lax
import numpy as np

D_MODEL = 1024
BATCH = 8
SEQ = 4096
DEPTH = 2

N_META = 16
D_FF = 2816
EPS = 1e-6
FOURIER_GROUPS = 8
FOURIER_GROUP_DIM = 64
D_FOURIER = FOURIER_GROUPS * FOURIER_GROUP_DIM
D_SCONV = 512
SCONV_WIDTH = 3
DN_HEADS = 4
DN_HEAD_DIM = 128
D_DN = DN_HEADS * DN_HEAD_DIM
DN_CONV_WIDTH = 3
CHUNK = 64
N_BRANCH = 3
PROJ_SIZES = (D_FOURIER, 3 * D_SCONV, 3 * D_DN, D_DN, 2 * DN_HEADS, 2 * DN_HEADS, N_BRANCH * D_MODEL)
D_IN_PROJ = D_FOURIER + 3 * D_SCONV + 3 * D_DN + D_DN + 4 * DN_HEADS + N_BRANCH * D_MODEL

kernel_name = "hybrid_fourier_shortconv_gdn_encoder"


def rmsnorm(x, g):
    xf = x.astype(jnp.float32)
    y = xf * lax.rsqrt(jnp.mean(xf * xf, axis=-1, keepdims=True) + EPS)
    return (y * g.astype(jnp.float32)).astype(x.dtype)


def swiglu(x, wi, wo):
    a, b = jnp.split(x @ wi, 2, axis=-1)
    return (jax.nn.silu(a) * b) @ wo


def conv3_centred(x, w):
    xp = jnp.pad(x, ((0, 0), (1, 1), (0, 0)))
    return w[0] * xp[:, :-2] + w[1] * xp[:, 1:-1] + w[2] * xp[:, 2:]


def fourier_mix(u):
    B, L, _ = u.shape
    ug = u.astype(jnp.float32).reshape(B, L, FOURIER_GROUPS, FOURIER_GROUP_DIM)
    y = jnp.fft.fft2(ug, axes=(1, 3), norm="ortho").real
    return y.reshape(B, L, D_FOURIER).astype(u.dtype)


def short_conv_mix(u, w):
    b, c, h = jnp.split(u, 3, axis=-1)
    return b * conv3_centred(c * h, w)


def l2norm(x):
    return x * lax.rsqrt(jnp.sum(x * x, axis=-1, keepdims=True) + EPS)


def chunk_gated_delta(q, k, v, beta, g):
    B, H, N, C, DK = q.shape
    DV = v.shape[-1]
    lower = jnp.tril(jnp.ones((C, C), bool))
    strict = jnp.tril(jnp.ones((C, C), bool), -1)
    gc = jnp.cumsum(g, axis=-1)
    diff = gc[..., :, None] - gc[..., None, :]
    decay_mask = jnp.exp(jnp.where(lower, diff, -jnp.inf))
    k_beta = k * beta[..., None]
    v_beta = v * beta[..., None]
    m = jnp.where(strict, jnp.einsum('bhnck,bhnsk->bhncs', k_beta, k) * decay_mask, 0.0)
    eye = jnp.eye(C, dtype=q.dtype)
    t = lax.linalg.triangular_solve(eye + m, jnp.broadcast_to(eye, m.shape), left_side=True,
                                    lower=True, unit_diagonal=True)
    u = t @ v_beta
    w = t @ (k_beta * jnp.exp(gc)[..., None])
    attn = jnp.einsum('bhnck,bhnsk->bhncs', q, k) * decay_mask
    q_dec = q * jnp.exp(gc)[..., None]
    k_dec = k * jnp.exp(gc[..., -1:] - gc)[..., None]
    g_last = jnp.exp(gc[..., -1])

    def step(S, xs):
        attn_n, u_n, w_n, qd_n, kd_n, gl_n = xs
        v_new = u_n - w_n @ S
        o = qd_n @ S + attn_n @ v_new
        S = S * gl_n[..., None, None] + jnp.swapaxes(kd_n, -1, -2) @ v_new
        return S, o

    xs = tuple(jnp.moveaxis(a, 2, 0) for a in (attn, u, w, q_dec, k_dec, g_last))
    S0 = jnp.zeros((B, H, DK, DV), q.dtype)
    _, o = lax.scan(step, S0, xs)
    return jnp.moveaxis(o, 0, 2)


def deltanet_branch(qkv, z, beta_logit, alpha_logit, conv_w, A_log, dt_bias, norm_w):
    B, L, _ = qkv.shape
    f32 = jnp.float32
    qkv = jax.nn.silu(conv3_centred(qkv, conv_w)).astype(f32)
    q, k, v = jnp.split(qkv, 3, axis=-1)
    q = l2norm(q.reshape(B, L, DN_HEADS, DN_HEAD_DIM)) * (DN_HEAD_DIM ** -0.5)
    k = l2norm(k.reshape(B, L, DN_HEADS, DN_HEAD_DIM))
    v = v.reshape(B, L, DN_HEADS, DN_HEAD_DIM)
    beta = jax.nn.sigmoid(beta_logit.astype(f32)).reshape(B, L, 2, DN_HEADS)
    g = -jnp.exp(A_log.astype(f32)) * jax.nn.softplus(
        alpha_logit.astype(f32).reshape(B, L, 2, DN_HEADS) + dt_bias.astype(f32))
    pad = CHUNK - N_META
    padt = lambda a: jnp.pad(a, ((0, 0), (pad, 0)) + ((0, 0),) * (a.ndim - 2))
    q, k, v, beta, g = padt(q), padt(k), padt(v), padt(beta), padt(g)
    Lp = L + pad
    N = Lp // CHUNK

    def chunks4(a):
        return a.reshape(B, N, CHUNK, DN_HEADS, -1).transpose(0, 3, 1, 2, 4)

    def chunks3(a):
        return a.reshape(B, N, CHUNK, DN_HEADS).transpose(0, 3, 1, 2)

    def unchunk(o):
        return o.transpose(0, 2, 3, 1, 4).reshape(B, Lp, DN_HEADS, DN_HEAD_DIM)

    o_fwd = unchunk(chunk_gated_delta(chunks4(q), chunks4(k), chunks4(v),
                                      chunks3(beta[:, :, 0]), chunks3(g[:, :, 0])))
    fl = lambda a: jnp.flip(a, axis=1)
    o_bwd = fl(unchunk(chunk_gated_delta(chunks4(fl(q)), chunks4(fl(k)), chunks4(fl(v)),
                                         chunks3(fl(beta[:, :, 1])), chunks3(fl(g[:, :, 1])))))
    o = (o_fwd + o_bwd)[:, pad:]
    o = o * lax.rsqrt(jnp.mean(o * o, axis=-1, keepdims=True) + EPS) * norm_w.astype(f32)
    o = o * jax.nn.silu(z.astype(f32).reshape(B, L, DN_HEADS, DN_HEAD_DIM))
    return o.reshape(B, L, D_DN).astype(z.dtype)


def hybrid_mixer(h, w_in, sconv_w, dn_conv_w, dn_A_log, dn_dt_bias, dn_norm,
                 w_fourier, w_sconv_out, w_dn_out, w_out):
    B, L, D = h.shape
    proj = h @ w_in
    idx = [int(s) for s in np.cumsum(PROJ_SIZES)[:-1]]
    u_f, u_sc, u_qkv, z, b_logit, a_logit, gate_logit = jnp.split(proj, idx, axis=-1)
    y_f = fourier_mix(u_f) @ w_fourier
    y_sc = short_conv_mix(u_sc, sconv_w) @ w_sconv_out
    y_dn = deltanet_branch(u_qkv, z, b_logit, a_logit, dn_conv_w, dn_A_log, dn_dt_bias,
                           dn_norm) @ w_dn_out
    gates = jax.nn.sigmoid(gate_logit).reshape(B, L, N_BRANCH, D)
    merged = gates[:, :, 0] * y_f + gates[:, :, 1] * y_sc + gates[:, :, 2] * y_dn
    return merged @ w_out


def setup_inputs(seed: int = 0) -> dict:
    key = jax.random.key(seed)
    ks = jax.random.split(key, 20)
    f32 = jnp.float32

    def dense(k, shape, fan_in):
        return jax.random.normal(k, shape, f32) * fan_in ** -0.5

    x = jax.random.normal(ks[0], (BATCH, SEQ, D_MODEL), f32)
    meta_tokens = jax.random.normal(ks[1], (N_META, D_MODEL), f32)
    norm_gains = 1.0 + 0.05 * jax.random.normal(ks[2], (DEPTH, 6, D_MODEL), f32)
    ffn1_wi = dense(ks[3], (DEPTH, D_MODEL, 2 * D_FF), D_MODEL)
    ffn1_wo = dense(ks[4], (DEPTH, D_FF, D_MODEL), D_FF)
    ffn2_wi = dense(ks[5], (DEPTH, D_MODEL, 2 * D_FF), D_MODEL)
    ffn2_wo = dense(ks[6], (DEPTH, D_FF, D_MODEL), D_FF)
    w_in = dense(ks[7], (DEPTH, D_MODEL, D_IN_PROJ), D_MODEL)
    sconv_w = dense(ks[8], (DEPTH, SCONV_WIDTH, D_SCONV), SCONV_WIDTH)
    dn_conv_w = dense(ks[9], (DEPTH, DN_CONV_WIDTH, 3 * D_DN), DN_CONV_WIDTH)
    dn_A_log = jnp.log(jax.random.uniform(ks[10], (DEPTH, 2, DN_HEADS), f32, 1.0, 16.0))
    dt = jnp.exp(jax.random.uniform(ks[11], (DEPTH, 2, DN_HEADS), f32,
                                    math.log(1e-3), math.log(1e-1)))
    dn_dt_bias = dt + jnp.log(-jnp.expm1(-dt))
    dn_norm = 1.0 + 0.05 * jax.random.normal(ks[12], (DEPTH, DN_HEAD_DIM), f32)
    w_fourier = dense(ks[13], (DEPTH, D_FOURIER, D_MODEL), D_FOURIER)
    w_sconv_out = dense(ks[14], (DEPTH, D_SCONV, D_MODEL), D_SCONV)
    w_dn_out = dense(ks[15], (DEPTH, D_DN, D_MODEL), D_DN)
    w_out = dense(ks[16], (DEPTH, D_MODEL, D_MODEL), D_MODEL)
    return {"x": x, "meta_tokens": meta_tokens, "norm_gains": norm_gains,
            "ffn1_wi": ffn1_wi, "ffn1_wo": ffn1_wo, "ffn2_wi": ffn2_wi, "ffn2_wo": ffn2_wo,
            "w_in": w_in, "sconv_w": sconv_w, "dn_conv_w": dn_conv_w, "dn_A_log": dn_A_log,
            "dn_dt_bias": dn_dt_bias, "dn_norm": dn_norm, "w_fourier": w_fourier,
            "w_sconv_out": w_sconv_out, "w_dn_out": w_dn_out, "w_out": w_out}


def reference(x, meta_tokens, norm_gains, ffn1_wi, ffn1_wo, ffn2_wi, ffn2_wo, w_in, sconv_w,
              dn_conv_w, dn_A_log, dn_dt_bias, dn_norm, w_fourier, w_sconv_out, w_dn_out, w_out):
    B = x.shape[0]
    meta = jnp.broadcast_to(meta_tokens[None].astype(x.dtype), (B, N_META, D_MODEL))
    h = jnp.concatenate([meta, x], axis=1)
    for l in range(DEPTH):
        n = norm_gains[l]
        h = h + 0.5 * rmsnorm(swiglu(rmsnorm(h, n[0]), ffn1_wi[l], ffn1_wo[l]), n[1])
        h = h + rmsnorm(hybrid_mixer(rmsnorm(h, n[2]), w_in[l], sconv_w[l], dn_conv_w[l],
                                     dn_A_log[l], dn_dt_bias[l], dn_norm[l], w_fourier[l],
                                     w_sconv_out[l], w_dn_out[l], w_out[l]), n[3])
        h = h + 0.5 * rmsnorm(swiglu(rmsnorm(h, n[4]), ffn2_wi[l], ffn2_wo[l]), n[5])
    return h[:, N_META:]
```

```python
import functools
import math

import numpy as np
import jax
import jax.numpy as jnp
from jax import lax
from jax.experimental import pallas as pl
from jax.experimental.pallas import tpu as pltpu

EPS = 1e-6
N_META = 16
CHUNK = 64
FRONT_PAD = CHUNK - N_META
DN_HEADS = 4
DN_HEAD_DIM = 128
FOURIER_GROUP_DIM = 64
N_BRANCH = 3
LANE = 128
BF16 = jnp.bfloat16
F32 = jnp.float32
VMEM_LIMIT_BYTES = 58 * 1024 * 1024
HIGHEST = lax.Precision.HIGHEST

_NT = (((1,), (1,)), ((), ()))
_TN = (((0,), (0,)), ((), ()))


def _dot(a, b, precision=None):
    return jnp.dot(a, b, preferred_element_type=F32, precision=precision)


def _rms(x, g):
    ms = jnp.mean(x * x, axis=-1, keepdims=True)
    return x * lax.rsqrt(ms + EPS) * g


def _silu(x):
    return x * jax.nn.sigmoid(x)


def _params(n_axes):
    return pltpu.CompilerParams(dimension_semantics=("arbitrary",) * n_axes,
                                vmem_limit_bytes=VMEM_LIMIT_BYTES)


def _resident(shape):
    zeros = (0,) * len(shape)
    return pl.BlockSpec(shape, lambda *_: zeros, pipeline_mode=pl.Buffered(1))


def _ffn_kernel(h_ref, gpre_ref, gpost_ref, wi_ref, wo_ref, o_ref, hm_ref, *, d_ff, fch):
    x = h_ref[...]
    xn = _rms(x, gpre_ref[...]).astype(BF16)
    for c in range(d_ff // fch):
        a = _dot(xn, wi_ref[:, c * fch:(c + 1) * fch])
        b = _dot(xn, wi_ref[:, d_ff + c * fch:d_ff + (c + 1) * fch])
        hm_ref[:, c * fch:(c + 1) * fch] = (_silu(a) * b).astype(BF16)
    y = _dot(hm_ref[...], wo_ref[...])
    o_ref[...] = x + 0.5 * _rms(y, gpost_ref[...])


def _ffn(h2d, gpre, gpost, wi, wo, tm):
    t, d = h2d.shape
    d_ff = wo.shape[0]
    fch = d_ff // 2 if (d_ff // 2) % LANE == 0 else d_ff
    return pl.pallas_call(
        functools.partial(_ffn_kernel, d_ff=d_ff, fch=fch),
        grid=(t // tm,),
        in_specs=[pl.BlockSpec((tm, d), lambda i: (i, 0)),
                  _resident((1, d)), _resident((1, d)),
                  _resident((d, 2 * d_ff)), _resident((d_ff, d))],
        out_specs=pl.BlockSpec((tm, d), lambda i: (i, 0)),
        out_shape=jax.ShapeDtypeStruct((t, d), F32),
        scratch_shapes=[pltpu.VMEM((tm, d_ff), BF16)],
        compiler_params=_params(1),
        name="ffn",
    )(h2d, gpre, gpost, wi, wo)


def _inproj_kernel(h_ref, g_ref, wmain_ref, wlog_ref, wgate_ref, wcs_ref,
                   ab_ref, bsc_ref, ch_ref, qkv_ref, z_ref, log_ref, gate_ref, *, d_f, d_sc, d_dn):
    xn = _rms(h_ref[...], g_ref[...]).astype(BF16)
    o = 0
    uf = _dot(xn, wmain_ref[:, o:o + d_f])
    ab_ref[...] = _dot(uf.astype(BF16), wcs_ref[...]).astype(BF16)
    o += d_f
    usc = _dot(xn, wmain_ref[:, o:o + 3 * d_sc])
    bsc_ref[...] = usc[:, :d_sc].astype(BF16)
    ch_ref[...] = (usc[:, d_sc:2 * d_sc] * usc[:, 2 * d_sc:]).astype(BF16)
    o += 3 * d_sc
    qkv_ref[...] = _dot(xn, wmain_ref[:, o:o + 3 * d_dn]).astype(BF16)
    o += 3 * d_dn
    z_ref[...] = _dot(xn, wmain_ref[:, o:o + d_dn]).astype(BF16)
    log_ref[...] = _dot(xn, wlog_ref[...])
    gate_ref[...] = jax.nn.sigmoid(_dot(xn, wgate_ref[...])).astype(BF16)


def _inproj(h2d, g, wmain, wlog, wgate, wcs, tm, d_f, d_sc, d_dn):
    t, d = h2d.shape
    row = lambda n: pl.BlockSpec((tm, n), lambda i: (i, 0))
    outs = [(2 * d_f, BF16), (d_sc, BF16), (d_sc, BF16), (3 * d_dn, BF16), (d_dn, BF16),
            (LANE, F32), (wgate.shape[1], BF16)]
    return pl.pallas_call(
        functools.partial(_inproj_kernel, d_f=d_f, d_sc=d_sc, d_dn=d_dn),
        grid=(t // tm,),
        in_specs=[row(d), _resident((1, d)), _resident(wmain.shape), _resident(wlog.shape),
                  _resident(wgate.shape), _resident(wcs.shape)],
        out_specs=[row(n) for n, _ in outs],
        out_shape=[jax.ShapeDtypeStruct((t, n), dt) for n, dt in outs],
        compiler_params=_params(1),
        name="inproj",
    )(h2d, g, wmain, wlog, wgate, wcs)


def _dft_kernel(pc_ref, ps_ref, qc_ref, qs_ref, a_ref, b_ref, y_ref, c_scr, s_scr, *, scale):
    @pl.when(pl.program_id(1) == 0)
    def _():
        pc = pc_ref[...]
        ps = ps_ref[...]
        for nb in range(c_scr.shape[1] // LANE):
            qc = qc_ref[:, nb:nb + 1]
            qs = qs_ref[:, nb:nb + 1]
            c_scr[:, nb * LANE:(nb + 1) * LANE] = (qc * pc - qs * ps).astype(BF16)
            s_scr[:, nb * LANE:(nb + 1) * LANE] = (qs * pc + qc * ps).astype(BF16)

    y = _dot(c_scr[...], a_ref[0]) + _dot(s_scr[...], b_ref[0])
    y_ref[0] = (y * scale).astype(BF16)


def _dft_seeds(lp, l_real):
    k = jnp.arange(lp, dtype=jnp.int32) - FRONT_PAD
    valid = (k >= 0) & (k < l_real)
    kk = jnp.where(valid, k, 0)[:, None]
    theta = 2.0 * math.pi / l_real
    j = jnp.arange(LANE, dtype=jnp.int32)[None, :]
    rp = ((kk * j) % l_real).astype(F32) * theta
    nb = lp // LANE
    ob = (jnp.arange(LANE, dtype=jnp.int32) * LANE - FRONT_PAD) % l_real
    rq = ((kk * ob[None, :]) % l_real).astype(F32) * theta
    qmask = valid[:, None] & (jnp.arange(LANE)[None, :] < nb)
    return (jnp.cos(rp), jnp.sin(rp),
            jnp.where(qmask, jnp.cos(rq), 0.0), jnp.where(qmask, jnp.sin(rq), 0.0))


def _dft(ab, seeds, tmf, d_f, scale):
    bsz, lp, _ = ab.shape
    pc, ps, qc, qs = seeds
    seed = pl.BlockSpec((tmf, LANE), lambda m, b: (m, 0))
    return pl.pallas_call(
        functools.partial(_dft_kernel, scale=scale),
        grid=(lp // tmf, bsz),
        in_specs=[seed, seed, seed, seed,
                  pl.BlockSpec((1, lp, d_f), lambda m, b: (b, 0, 0)),
                  pl.BlockSpec((1, lp, d_f), lambda m, b: (b, 0, 1))],
        out_specs=pl.BlockSpec((1, tmf, d_f), lambda m, b: (b, m, 0)),
        out_shape=jax.ShapeDtypeStruct((bsz, lp, d_f), BF16),
        scratch_shapes=[pltpu.VMEM((tmf, lp), BF16), pltpu.VMEM((tmf, lp), BF16)],
        compiler_params=_params(2),
        name="dft",
    )(pc, ps, qc, qs, ab, ab)


def _softplus(x):
    return jnp.maximum(x, 0.0) + jnp.log1p(jnp.exp(-jnp.abs(x)))


def _shifted_rows(src, s, br, lp):
    x = src[0, pl.ds(s, br), :].astype(F32)
    sp = jnp.maximum(s - 16, 0)
    sn = jnp.minimum(s + br, lp - 16)
    xp = src[0, pl.ds(pl.multiple_of(sp, 16), 16), :].astype(F32)[15:16]
    xn = src[0, pl.ds(pl.multiple_of(sn, 16), 16), :].astype(F32)[0:1]
    xp = jnp.where(s > 0, xp, 0.0)
    xn = jnp.where(s + br < lp, xn, 0.0)
    rows = lax.broadcasted_iota(jnp.int32, (br, 1), 0)
    x_prev = jnp.where(rows == 0, xp, pltpu.roll(x, 1, 0))
    x_next = jnp.where(rows == br - 1, xn, pltpu.roll(x, br - 1, 0))
    return x_prev, x, x_next


def _dn_kernel(q_ref, k_ref, v_ref, z_ref, lg_ref, cwq_ref, cwk_ref, cwv_ref, par_ref, nw_ref,
               o_ref, qs, ks, vs, bg, aq, ms, bo, bs, gl, *, lp, l_real, group):
    hd = pl.program_id(1)
    c_sz, d = CHUNK, DN_HEAD_DIM
    nc = lp // c_sz
    br = LANE

    lane_row = lax.broadcasted_iota(jnp.int32, (1, LANE), 1)

    def prep(i, carry):
        s = pl.multiple_of(i * br, br)
        for src, cw, dst, kind in ((q_ref, cwq_ref, qs, "q"), (k_ref, cwk_ref, ks, "k"),
                                   (v_ref, cwv_ref, vs, "v")):
            x_prev, x, x_next = _shifted_rows(src, s, br, lp)
            w = cw[...]
            y = _silu(w[0:1] * x_prev + w[1:2] * x + w[2:3] * x_next)
            if kind != "v":
                y = y * lax.rsqrt(jnp.sum(y * y, axis=-1, keepdims=True) + EPS)
            if kind == "q":
                y = y * (d ** -0.5)
            dst[pl.ds(s, br), :] = y
        lgt = lg_ref[0, pl.ds(s, br), :]
        beta = jax.nn.sigmoid(lgt)
        g = -jnp.exp(par_ref[0:1, :]) * _softplus(lgt + par_ref[1:2, :])
        grow = s + lax.broadcasted_iota(jnp.int32, (br, 1), 0)
        valid = (grow >= FRONT_PAD) & (grow < FRONT_PAD + l_real)
        bgv = jnp.where(lane_row < 2 * DN_HEADS, beta, g)
        bg[pl.ds(s, br), :] = jnp.where(valid, bgv, 0.0)
        return carry

    lax.fori_loop(0, lp // br, prep, 0)

    ii = lax.broadcasted_iota(jnp.int32, (c_sz, c_sz), 0)
    jj = lax.broadcasted_iota(jnp.int32, (c_sz, c_sz), 1)
    ltri = (ii >= jj).astype(F32)
    utri = (ii <= jj).astype(F32)
    eye = (ii == jj).astype(F32)
    lane_sel = [(lane_row == off + hd).astype(F32) for off in range(0, 4 * DN_HEADS, DN_HEADS)]

    def chunk(c):
        r0 = pl.multiple_of(c * c_sz, c_sz)
        rm = pl.multiple_of(c * d, d)
        qc = qs[pl.ds(r0, c_sz), :]
        kc = ks[pl.ds(r0, c_sz), :]
        vc = vs[pl.ds(r0, c_sz), :]
        bgc = bg[pl.ds(r0, c_sz), :]
        pre = _dot(ltri, bgc, HIGHEST)
        suf = _dot(utri, bgc, HIGHEST)
        gram = lax.dot_general(jnp.concatenate([qc, kc], axis=0).astype(BF16), kc.astype(BF16), _NT,
                               preferred_element_type=F32)
        gqk, gkk = gram[:c_sz], gram[c_sz:]
        for di in range(2):
            cs = pre if di == 0 else suf
            e_b, e_g = lane_sel[di], lane_sel[2 + di]
            gcol = jnp.sum(cs * e_g, axis=1, keepdims=True)
            grow = lax.dot_general(jnp.broadcast_to(e_g, (c_sz, LANE)), cs, _NT,
                                   preferred_element_type=F32, precision=HIGHEST)
            beta = jnp.sum(bgc * e_b, axis=1, keepdims=True)
            tri = (ii >= jj) if di == 0 else (ii <= jj)
            stri = (ii > jj) if di == 0 else (ii < jj)
            dec = jnp.where(tri, jnp.exp(jnp.minimum(gcol - grow, 0.0)), 0.0)
            nm = jnp.where(stri, -(beta * gkk * dec), 0.0)
            s_acc = eye + nm
            nmb = nm.astype(BF16)
            qpow = _dot(nmb, nmb)
            for _ in range(4):
                qb = qpow.astype(BF16)
                r = _dot(jnp.concatenate([s_acc, qpow], axis=0).astype(BF16), qb)
                s_acc = s_acc + r[:c_sz]
                qpow = r[c_sz:]
            tinv = s_acc + _dot(s_acc.astype(BF16), qpow.astype(BF16))
            eg = jnp.exp(gcol)
            gtot = gcol[c_sz - 1:c_sz] if di == 0 else gcol[0:1]
            rhs = jnp.concatenate([vc * beta, kc * (beta * eg)], axis=1).astype(BF16)
            uw = _dot(tinv.astype(BF16), rhs).astype(BF16)
            x1 = _dot((gqk * dec).astype(BF16), uw)
            kd = (kc * jnp.exp(gtot - gcol)).astype(BF16)
            x2 = lax.dot_general(kd, uw, _TN, preferred_element_type=F32)
            aq[di, pl.ds(r0, c_sz), :] = (qc * eg - x1[:, d:]).astype(BF16)
            bo[di, pl.ds(r0, c_sz), :] = x1[:, :d]
            ms[di, pl.ds(rm, d), :] = (-x2[:, d:]).astype(BF16)
            bs[di, pl.ds(rm, d), :] = x2[:, :d]
            gl[di, pl.ds(pl.multiple_of(c * 8, 8), 8), :] = jnp.broadcast_to(jnp.exp(gtot), (8, LANE))

    def phase1(i, carry):
        for j in range(group):
            chunk(i * group + j)
        return carry

    lax.fori_loop(0, nc // group, phase1, 0)

    def scan(t, carry):
        new = []
        for di, c, s_st in ((0, t, carry[0]), (1, nc - 1 - t, carry[1])):
            r0 = pl.multiple_of(c * c_sz, c_sz)
            rm = pl.multiple_of(c * d, d)
            lhs = jnp.concatenate([aq[di, pl.ds(r0, c_sz), :], ms[di, pl.ds(rm, d), :]], axis=0)
            r = _dot(lhs, s_st.astype(BF16))
            bo[di, pl.ds(r0, c_sz), :] = r[:c_sz] + bo[di, pl.ds(r0, c_sz), :]
            g_last = gl[di, pl.ds(pl.multiple_of(c * 8, 8), 8), :][0:1]
            new.append(g_last * s_st + r[c_sz:] + bs[di, pl.ds(rm, d), :])
        return tuple(new)

    zero = jnp.zeros((d, d), F32)
    lax.fori_loop(0, nc, scan, (zero, zero))

    def fin(i, carry):
        s = pl.multiple_of(i * br, br)
        o = bo[0, pl.ds(s, br), :] + bo[1, pl.ds(s, br), :]
        o = o * lax.rsqrt(jnp.mean(o * o, axis=-1, keepdims=True) + EPS) * nw_ref[...]
        o_ref[0, pl.ds(s, br), :] = (o * _silu(z_ref[0, pl.ds(s, br), :].astype(F32))).astype(BF16)
        return carry

    lax.fori_loop(0, lp // br, fin, 0)


def _deltanet(qkv, z, logits, conv_w, par, norm_w, l_real):
    bsz, lp, _ = qkv.shape
    d, nh = DN_HEAD_DIM, DN_HEADS
    nc = lp // CHUNK
    group = next(g for g in (6, 4, 3, 2, 1) if nc % g == 0)
    head = lambda off: pl.BlockSpec((1, lp, d), lambda b, h: (b, 0, off + h))
    cw = lambda off: pl.BlockSpec((conv_w.shape[0], d), lambda b, h: (0, off + h))
    return pl.pallas_call(
        functools.partial(_dn_kernel, lp=lp, l_real=l_real, group=group),
        grid=(bsz, nh),
        in_specs=[head(0), head(nh), head(2 * nh), head(0),
                  pl.BlockSpec((1, lp, LANE), lambda b, h: (b, 0, 0)),
                  cw(0), cw(nh), cw(2 * nh),
                  pl.BlockSpec(par.shape, lambda b, h: (0, 0)),
                  pl.BlockSpec((1, d), lambda b, h: (0, 0))],
        out_specs=head(0),
        out_shape=jax.ShapeDtypeStruct((bsz, lp, nh * d), BF16),
        scratch_shapes=[pltpu.VMEM((lp, d), F32), pltpu.VMEM((lp, d), F32), pltpu.VMEM((lp, d), F32),
                        pltpu.VMEM((lp, LANE), F32),
                        pltpu.VMEM((2, lp, d), BF16), pltpu.VMEM((2, nc * d, d), BF16),
                        pltpu.VMEM((2, lp, d), F32), pltpu.VMEM((2, nc * d, d), F32),
                        pltpu.VMEM((2, nc * 8, LANE), F32)],
        compiler_params=_params(2),
        name="deltanet",
    )(qkv, qkv, qkv, z, logits, conv_w, conv_w, conv_w, par, norm_w)


def _merge_kernel(h_ref, y_ref, b_ref, ch_ref, chp_ref, chn_ref, dn_ref, gate_ref, sw_ref,
                  wf_ref, wsc_ref, wdn_ref, wout_ref, gpost_ref, o_ref, *, d_model):
    tm = h_ref.shape[0]
    ch = ch_ref[...].astype(F32)
    rows = lax.broadcasted_iota(jnp.int32, (tm, 1), 0)
    i = pl.program_id(0)
    halo_prev = jnp.where(i > 0, chp_ref[...].astype(F32)[15:16], 0.0)
    halo_next = jnp.where(i < pl.num_programs(0) - 1, chn_ref[...].astype(F32)[0:1], 0.0)
    ch_prev = jnp.where(rows == 0, halo_prev, pltpu.roll(ch, 1, 0))
    ch_next = jnp.where(rows == tm - 1, halo_next, pltpu.roll(ch, tm - 1, 0))
    sw = sw_ref[...]
    sc = b_ref[...].astype(F32) * (sw[0:1] * ch_prev + sw[1:2] * ch + sw[2:3] * ch_next)
    y_f = _dot(y_ref[...], wf_ref[...])
    y_sc = _dot(sc.astype(BF16), wsc_ref[...])
    y_dn = _dot(dn_ref[...], wdn_ref[...])
    merged = (gate_ref[:, 0:d_model].astype(F32) * y_f
              + gate_ref[:, d_model:2 * d_model].astype(F32) * y_sc
              + gate_ref[:, 2 * d_model:3 * d_model].astype(F32) * y_dn)
    out = _dot(merged.astype(BF16), wout_ref[...])
    o_ref[...] = h_ref[...] + _rms(out, gpost_ref[...])


def _merge(h2d, y, bsc, ch, dn, gate, sw, wf, wsc, wdn, wout, gpost, tm):
    t, d = h2d.shape
    row = lambda n: pl.BlockSpec((tm, n), lambda i: (i, 0))
    d_sc = ch.shape[1]
    hb = tm // 16
    last = t // 16 - 1
    return pl.pallas_call(
        functools.partial(_merge_kernel, d_model=d),
        grid=(t // tm,),
        in_specs=[row(d), row(y.shape[1]), row(d_sc), row(d_sc),
                  pl.BlockSpec((16, d_sc), lambda i: (jnp.maximum(i * hb - 1, 0), 0)),
                  pl.BlockSpec((16, d_sc), lambda i: (jnp.minimum((i + 1) * hb, last), 0)),
                  row(dn.shape[1]), row(gate.shape[1]),
                  _resident(sw.shape), _resident(wf.shape), _resident(wsc.shape), _resident(wdn.shape),
                  _resident(wout.shape), _resident((1, d))],
        out_specs=row(d),
        out_shape=jax.ShapeDtypeStruct((t, d), F32),
        compiler_params=_params(1),
        name="merge",
    )(h2d, y, bsc, ch, ch, ch, dn, gate, sw, wf, wsc, wdn, wout, gpost)


def _channel_dft_matrix(d_f):
    gd = FOURIER_GROUP_DIM
    idx = np.arange(gd)
    ang = 2.0 * np.pi * ((idx[:, None] * idx[None, :]) % gd) / gd
    blk = np.eye(d_f // gd)
    return np.concatenate([np.kron(blk, np.cos(ang)), np.kron(blk, -np.sin(ang))], axis=1)


def _pick_tile(n, candidates):
    return next(c for c in candidates if n % c == 0)


def kernel(x, meta_tokens, norm_gains, ffn1_wi, ffn1_wo, ffn2_wi, ffn2_wo, w_in, sconv_w, dn_conv_w,
           dn_A_log, dn_dt_bias, dn_norm, w_fourier, w_sconv_out, w_dn_out, w_out):
    bsz, seq, d = x.shape
    depth = norm_gains.shape[0]
    d_f = w_fourier.shape[1]
    d_sc = w_sconv_out.shape[1]
    d_dn = w_dn_out.shape[1]
    n_log = 4 * DN_HEADS
    l_real = N_META + seq
    lp = -(-(FRONT_PAD + l_real) // LANE) * LANE
    t = bsz * lp
    tm = _pick_tile(t, (1024, 768, 512, 384, 256, 128))
    tmf = _pick_tile(lp, (384, 256, 128))

    meta = jnp.broadcast_to(meta_tokens[None].astype(x.dtype), (bsz, N_META, d))
    h = jnp.concatenate([jnp.zeros((bsz, FRONT_PAD, d), x.dtype), meta, x,
                         jnp.zeros((bsz, lp - FRONT_PAD - l_real, d), x.dtype)], axis=1).reshape(t, d)

    wcs = jnp.asarray(_channel_dft_matrix(d_f), F32).astype(BF16)
    seeds = _dft_seeds(lp, l_real)
    scale = 1.0 / math.sqrt(l_real * FOURIER_GROUP_DIM)
    o_main = d_f + 3 * d_sc + 4 * d_dn

    for l in range(depth):
        gains = norm_gains[l][:, None, :]
        h = _ffn(h, gains[0], gains[1], ffn1_wi[l].astype(BF16), ffn1_wo[l].astype(BF16), tm)

        w = w_in[l]
        wlog = jnp.pad(w[:, o_main:o_main + n_log], ((0, 0), (0, LANE - n_log))).astype(BF16)
        ab, bsc, ch, qkv, z, logits, gate = _inproj(
            h, gains[2], w[:, :o_main].astype(BF16), wlog, w[:, o_main + n_log:].astype(BF16), wcs,
            tm, d_f, d_sc, d_dn)
        y = _dft(ab.reshape(bsz, lp, 2 * d_f), seeds, tmf, d_f, scale)
        par = jnp.zeros((8, LANE), F32)
        par = par.at[0, 2 * DN_HEADS:n_log].set(dn_A_log[l].reshape(-1))
        par = par.at[1, 2 * DN_HEADS:n_log].set(dn_dt_bias[l].reshape(-1))
        dn = _deltanet(qkv.reshape(bsz, lp, 3 * d_dn), z.reshape(bsz, lp, d_dn),
                       logits.reshape(bsz, lp, LANE), dn_conv_w[l], par, dn_norm[l][None, :], l_real)
        h = _merge(h, y.reshape(t, d_f), bsc, ch, dn.reshape(t, d_dn), gate, sconv_w[l],
                   w_fourier[l].astype(BF16), w_sconv_out[l].astype(BF16), w_dn_out[l].astype(BF16),
                   w_out[l].astype(BF16), gains[3], tm)

        h = _ffn(h, gains[4], gains[5], ffn2_wi[l].astype(BF16), ffn2_wo[l].astype(BF16), tm)

    return h.reshape(bsz, lp, d)[:, FRONT_PAD + N_META:FRONT_PAD + l_real]
```

```python
import functools
import math

import numpy as np
import jax
import jax.numpy as jnp
from jax import lax
from jax.experimental import pallas as pl
from jax.experimental.pallas import tpu as pltpu

EPS = 1e-6
N_META = 16
CHUNK = 64
FRONT_PAD = CHUNK - N_META
DN_HEADS = 4
DN_HEAD_DIM = 128
FOURIER_GROUP_DIM = 64
N_BRANCH = 3
LANE = 128
BF16 = jnp.bfloat16
F32 = jnp.float32
VMEM_LIMIT_BYTES = 58 * 1024 * 1024
HIGHEST = lax.Precision.HIGHEST

_NT = (((1,), (1,)), ((), ()))
_TN = (((0,), (0,)), ((), ()))


def _dot(a, b, precision=None):
    return jnp.dot(a, b, preferred_element_type=F32, precision=precision)


def _rms(x, g):
    ms = jnp.mean(x * x, axis=-1, keepdims=True)
    return x * lax.rsqrt(ms + EPS) * g


def _silu(x):
    return x * jax.nn.sigmoid(x)


def _params(n_axes):
    return pltpu.CompilerParams(dimension_semantics=("arbitrary",) * n_axes,
                                vmem_limit_bytes=VMEM_LIMIT_BYTES)


def _resident(shape):
    zeros = (0,) * len(shape)
    return pl.BlockSpec(shape, lambda *_: zeros, pipeline_mode=pl.Buffered(1))


def _ffn_kernel(h_ref, gpre_ref, gpost_ref, wi_ref, wo_ref, o_ref, hm_ref, *, d_ff, fch):
    x = h_ref[...]
    xn = _rms(x, gpre_ref[...]).astype(BF16)
    for c in range(d_ff // fch):
        a = _dot(xn, wi_ref[:, c * fch:(c + 1) * fch])
        b = _dot(xn, wi_ref[:, d_ff + c * fch:d_ff + (c + 1) * fch])
        hm_ref[:, c * fch:(c + 1) * fch] = (_silu(a) * b).astype(BF16)
    y = _dot(hm_ref[...], wo_ref[...])
    o_ref[...] = x + 0.5 * _rms(y, gpost_ref[...])


def _ffn(h2d, gpre, gpost, wi, wo, tm):
    t, d = h2d.shape
    d_ff = wo.shape[0]
    fch = d_ff // 2 if (d_ff // 2) % LANE == 0 else d_ff
    return pl.pallas_call(
        functools.partial(_ffn_kernel, d_ff=d_ff, fch=fch),
        grid=(t // tm,),
        in_specs=[pl.BlockSpec((tm, d), lambda i: (i, 0)),
                  _resident((1, d)), _resident((1, d)),
                  _resident((d, 2 * d_ff)), _resident((d_ff, d))],
        out_specs=pl.BlockSpec((tm, d), lambda i: (i, 0)),
        out_shape=jax.ShapeDtypeStruct((t, d), F32),
        scratch_shapes=[pltpu.VMEM((tm, d_ff), BF16)],
        compiler_params=_params(1),
        name="ffn",
    )(h2d, gpre, gpost, wi, wo)


def _inproj_kernel(h_ref, g_ref, wmain_ref, wlog_ref, wgate_ref, wcs_ref,
                   ab_ref, bsc_ref, ch_ref, qkv_ref, z_ref, log_ref, gate_ref, *, d_f, d_sc, d_dn):
    xn = _rms(h_ref[...], g_ref[...]).astype(BF16)
    o = 0
    uf = _dot(xn, wmain_ref[:, o:o + d_f])
    ab_ref[...] = _dot(uf.astype(BF16), wcs_ref[...]).astype(BF16)
    o += d_f
    usc = _dot(xn, wmain_ref[:, o:o + 3 * d_sc])
    bsc_ref[...] = usc[:, :d_sc].astype(BF16)
    ch_ref[...] = (usc[:, d_sc:2 * d_sc] * usc[:, 2 * d_sc:]).astype(BF16)
    o += 3 * d_sc
    qkv_ref[...] = _dot(xn, wmain_ref[:, o:o + 3 * d_dn]).astype(BF16)
    o += 3 * d_dn
    z_ref[...] = _dot(xn, wmain_ref[:, o:o + d_dn]).astype(BF16)
    log_ref[...] = _dot(xn, wlog_ref[...])
    gate_ref[...] = jax.nn.sigmoid(_dot(xn, wgate_ref[...])).astype(BF16)


def _inproj(h2d, g, wmain, wlog, wgate, wcs, tm, d_f, d_sc, d_dn):
    t, d = h2d.shape
    row = lambda n: pl.BlockSpec((tm, n), lambda i: (i, 0))
    outs = [(2 * d_f, BF16), (d_sc, BF16), (d_sc, BF16), (3 * d_dn, BF16), (d_dn, BF16),
            (LANE, F32), (wgate.shape[1], BF16)]
    return pl.pallas_call(
        functools.partial(_inproj_kernel, d_f=d_f, d_sc=d_sc, d_dn=d_dn),
        grid=(t // tm,),
        in_specs=[row(d), _resident((1, d)), _resident(wmain.shape), _resident(wlog.shape),
                  _resident(wgate.shape), _resident(wcs.shape)],
        out_specs=[row(n) for n, _ in outs],
        out_shape=[jax.ShapeDtypeStruct((t, n), dt) for n, dt in outs],
        compiler_params=_params(1),
        name="inproj",
    )(h2d, g, wmain, wlog, wgate, wcs)


def _dft_kernel(pc_ref, ps_ref, qc_ref, qs_ref, a_ref, b_ref, y_ref, c_scr, s_scr, *, scale):
    @pl.when(pl.program_id(1) == 0)
    def _():
        pc = pc_ref[...]
        ps = ps_ref[...]
        for nb in range(c_scr.shape[1] // LANE):
            qc = qc_ref[:, nb:nb + 1]
            qs = qs_ref[:, nb:nb + 1]
            c_scr[:, nb * LANE:(nb + 1) * LANE] = (qc * pc - qs * ps).astype(BF16)
            s_scr[:, nb * LANE:(nb + 1) * LANE] = (qs * pc + qc * ps).astype(BF16)

    y = _dot(c_scr[...], a_ref[0]) + _dot(s_scr[...], b_ref[0])
    y_ref[0] = (y * scale).astype(BF16)


def _dft_seeds(lp, l_real):
    k = jnp.arange(lp, dtype=jnp.int32) - FRONT_PAD
    valid = (k >= 0) & (k < l_real)
    kk = jnp.where(valid, k, 0)[:, None]
    theta = 2.0 * math.pi / l_real
    j = jnp.arange(LANE, dtype=jnp.int32)[None, :]
    rp = ((kk * j) % l_real).astype(F32) * theta
    nb = lp // LANE
    ob = (jnp.arange(LANE, dtype=jnp.int32) * LANE - FRONT_PAD) % l_real
    rq = ((kk * ob[None, :]) % l_real).astype(F32) * theta
    qmask = valid[:, None] & (jnp.arange(LANE)[None, :] < nb)
    return (jnp.cos(rp), jnp.sin(rp),
            jnp.where(qmask, jnp.cos(rq), 0.0), jnp.where(qmask, jnp.sin(rq), 0.0))


def _dft(ab, seeds, tmf, d_f, scale):
    bsz, lp, _ = ab.shape
    pc, ps, qc, qs = seeds
    seed = pl.BlockSpec((tmf, LANE), lambda m, b: (m, 0))
    return pl.pallas_call(
        functools.partial(_dft_kernel, scale=scale),
        grid=(lp // tmf, bsz),
        in_specs=[seed, seed, seed, seed,
                  pl.BlockSpec((1, lp, d_f), lambda m, b: (b, 0, 0)),
                  pl.BlockSpec((1, lp, d_f), lambda m, b: (b, 0, 1))],
        out_specs=pl.BlockSpec((1, tmf, d_f), lambda m, b: (b, m, 0)),
        out_shape=jax.ShapeDtypeStruct((bsz, lp, d_f), BF16),
        scratch_shapes=[pltpu.VMEM((tmf, lp), BF16), pltpu.VMEM((tmf, lp), BF16)],
        compiler_params=_params(2),
        name="dft",
    )(pc, ps, qc, qs, ab, ab)


def _softplus(x):
    return jnp.maximum(x, 0.0) + jnp.log1p(jnp.exp(-jnp.abs(x)))


def _shifted_rows(src, s, br, lp):
    x = src[0, pl.ds(s, br), :].astype(F32)
    sp = jnp.maximum(s - 16, 0)
    sn = jnp.minimum(s + br, lp - 16)
    xp = src[0, pl.ds(pl.multiple_of(sp, 16), 16), :].astype(F32)[15:16]
    xn = src[0, pl.ds(pl.multiple_of(sn, 16), 16), :].astype(F32)[0:1]
    xp = jnp.where(s > 0, xp, 0.0)
    xn = jnp.where(s + br < lp, xn, 0.0)
    rows = lax.broadcasted_iota(jnp.int32, (br, 1), 0)
    x_prev = jnp.where(rows == 0, xp, pltpu.roll(x, 1, 0))
    x_next = jnp.where(rows == br - 1, xn, pltpu.roll(x, br - 1, 0))
    return x_prev, x, x_next


def _dn_kernel(q_ref, k_ref, v_ref, z_ref, lg_ref, cwq_ref, cwk_ref, cwv_ref, par_ref, nw_ref,
               o_ref, qs, ks, vs, bg, aq, ms, bo, bs, gl, *, lp, l_real, group):
    hd = pl.program_id(1)
    c_sz, d = CHUNK, DN_HEAD_DIM
    nc = lp // c_sz
    br = LANE

    lane_row = lax.broadcasted_iota(jnp.int32, (1, LANE), 1)

    def prep(i, carry):
        s = pl.multiple_of(i * br, br)
        for src, cw, dst, kind in ((q_ref, cwq_ref, qs, "q"), (k_ref, cwk_ref, ks, "k"),
                                   (v_ref, cwv_ref, vs, "v")):
            x_prev, x, x_next = _shifted_rows(src, s, br, lp)
            w = cw[...]
            y = _silu(w[0:1] * x_prev + w[1:2] * x + w[2:3] * x_next)
            if kind != "v":
                y = y * lax.rsqrt(jnp.sum(y * y, axis=-1, keepdims=True) + EPS)
            if kind == "q":
                y = y * (d ** -0.5)
            dst[pl.ds(s, br), :] = y
        lgt = lg_ref[0, pl.ds(s, br), :]
        beta = jax.nn.sigmoid(lgt)
        g = -jnp.exp(par_ref[0:1, :]) * _softplus(lgt + par_ref[1:2, :])
        grow = s + lax.broadcasted_iota(jnp.int32, (br, 1), 0)
        valid = (grow >= FRONT_PAD) & (grow < FRONT_PAD + l_real)
        bgv = jnp.where(lane_row < 2 * DN_HEADS, beta, g)
        bg[pl.ds(s, br), :] = jnp.where(valid, bgv, 0.0)
        return carry

    lax.fori_loop(0, lp // br, prep, 0)

    ii = lax.broadcasted_iota(jnp.int32, (c_sz, 2 * c_sz), 0)
    jl = lax.broadcasted_iota(jnp.int32, (c_sz, 2 * c_sz), 1)
    left = jl < c_sz
    jj = jnp.where(left, jl, jl - c_sz)
    causal = jnp.where(left, ii - jj, jj - ii)
    tri = causal >= 0
    stri = causal > 0
    eye2 = (ii == jj).astype(F32)
    i1 = lax.broadcasted_iota(jnp.int32, (c_sz, c_sz), 0)
    j1 = lax.broadcasted_iota(jnp.int32, (c_sz, c_sz), 1)
    lu = jnp.concatenate([(i1 >= j1).astype(F32), (i1 <= j1).astype(F32)], axis=0)
    ones = jnp.ones((c_sz, LANE), F32)
    zeros = jnp.zeros((c_sz, 2 * d), BF16)
    e_bf, e_bb, e_gf, e_gb = [(lane_row == off + hd).astype(F32)
                              for off in range(0, 4 * DN_HEADS, DN_HEADS)]

    def pair(x_f, x_b):
        return jnp.where(left, x_f, x_b)

    def blockdiag(x2):
        return jnp.concatenate([jnp.where(left, x2, 0.0), jnp.where(left, 0.0, x2)], axis=0).astype(BF16)

    def blockrows(x4):
        return jnp.concatenate([jnp.concatenate([x4[:, :2 * d], zeros], axis=1),
                                jnp.concatenate([zeros, x4[:, 2 * d:]], axis=1)], axis=0)

    def phase1(it, carry):
        cs_ = [it * group + j for j in range(group)]
        r0s = [pl.multiple_of(c * c_sz, c_sz) for c in cs_]
        qc = [qs[pl.ds(r0, c_sz), :] for r0 in r0s]
        kc = [ks[pl.ds(r0, c_sz), :] for r0 in r0s]
        vc = [vs[pl.ds(r0, c_sz), :] for r0 in r0s]
        bgc = [bg[pl.ds(r0, c_sz), :] for r0 in r0s]
        cum = [_dot(lu, x, HIGHEST) for x in bgc]
        gcf = [jnp.sum(x[:c_sz] * e_gf, axis=1, keepdims=True) for x in cum]
        gcb = [jnp.sum(x[c_sz:] * e_gb, axis=1, keepdims=True) for x in cum]
        btf = [jnp.sum(x * e_bf, axis=1, keepdims=True) for x in bgc]
        btb = [jnp.sum(x * e_bb, axis=1, keepdims=True) for x in bgc]
        grow = [lax.dot_general(ones, jnp.concatenate([x[:c_sz] * e_gf, x[c_sz:] * e_gb], axis=0), _NT,
                                preferred_element_type=F32, precision=HIGHEST) for x in cum]
        dec = [jnp.where(tri, jnp.exp(jnp.minimum(pair(f, b) - g, 0.0)), 0.0)
               for f, b, g in zip(gcf, gcb, grow)]
        gram = [lax.dot_general(jnp.concatenate([q, k], axis=0).astype(BF16),
                                jnp.concatenate([k, k], axis=0).astype(BF16), _NT,
                                preferred_element_type=F32) for q, k in zip(qc, kc)]
        nm = [jnp.where(stri, -(pair(f, b) * g[c_sz:] * dc), 0.0)
              for f, b, g, dc in zip(btf, btb, gram, dec)]
        sacc = [eye2 + x for x in nm]
        qpow = [_dot(x.astype(BF16), blockdiag(x)) for x in nm]
        for _ in range(4):
            r = [_dot(jnp.concatenate([s, q], axis=0).astype(BF16), blockdiag(q)) for s, q in zip(sacc, qpow)]
            sacc = [s + x[:c_sz] for s, x in zip(sacc, r)]
            qpow = [x[c_sz:] for x in r]
        tinv = [s + _dot(s.astype(BF16), blockdiag(q)) for s, q in zip(sacc, qpow)]
        egf = [jnp.exp(x) for x in gcf]
        egb = [jnp.exp(x) for x in gcb]
        rhs = [blockrows(jnp.concatenate([v * bf, k * (bf * ef), v * bb, k * (bb * eb)], axis=1).astype(BF16))
               for v, k, bf, bb, ef, eb in zip(vc, kc, btf, btb, egf, egb)]
        uw = [blockrows(_dot(t.astype(BF16), x).astype(BF16)) for t, x in zip(tinv, rhs)]
        x1 = [_dot((g[:c_sz] * dc).astype(BF16), x) for g, dc, x in zip(gram, dec, uw)]
        gtf = [x[c_sz - 1:c_sz] for x in gcf]
        gtb = [x[0:1] for x in gcb]
        kd = [jnp.concatenate([k * jnp.exp(tf - f), k * jnp.exp(tb - b)], axis=0).astype(BF16)
              for k, tf, tb, f, b in zip(kc, gtf, gtb, gcf, gcb)]
        x2 = [lax.dot_general(a, x, _TN, preferred_element_type=F32) for a, x in zip(kd, uw)]
        for j, c in enumerate(cs_):
            r0 = r0s[j]
            rm = pl.multiple_of(c * d, d)
            r8 = pl.multiple_of(c * 8, 8)
            for di, eg, gt in ((0, egf[j], gtf[j]), (1, egb[j], gtb[j])):
                o = 2 * d * di
                aq[di, pl.ds(r0, c_sz), :] = (qc[j] * eg - x1[j][:, o + d:o + 2 * d]).astype(BF16)
                bo[di, pl.ds(r0, c_sz), :] = x1[j][:, o:o + d]
                ms[di, pl.ds(rm, d), :] = (-x2[j][:, o + d:o + 2 * d]).astype(BF16)
                bs[di, pl.ds(rm, d), :] = x2[j][:, o:o + d]
                gl[di, pl.ds(r8, 8), :] = jnp.broadcast_to(jnp.exp(gt), (8, LANE))
        return carry

    lax.fori_loop(0, nc // group, phase1, 0)

    def scan(t, carry):
        new = []
        for di, c, s_st in ((0, t, carry[0]), (1, nc - 1 - t, carry[1])):
            r0 = pl.multiple_of(c * c_sz, c_sz)
            rm = pl.multiple_of(c * d, d)
            lhs = jnp.concatenate([aq[di, pl.ds(r0, c_sz), :], ms[di, pl.ds(rm, d), :]], axis=0)
            r = _dot(lhs, s_st.astype(BF16))
            bo[di, pl.ds(r0, c_sz), :] = r[:c_sz] + bo[di, pl.ds(r0, c_sz), :]
            g_last = gl[di, pl.ds(pl.multiple_of(c * 8, 8), 8), :][0:1]
            new.append(g_last * s_st + r[c_sz:] + bs[di, pl.ds(rm, d), :])
        return tuple(new)

    zero = jnp.zeros((d, d), F32)
    lax.fori_loop(0, nc, scan, (zero, zero))

    def fin(i, carry):
        s = pl.multiple_of(i * br, br)
        o = bo[0, pl.ds(s, br), :] + bo[1, pl.ds(s, br), :]
        o = o * lax.rsqrt(jnp.mean(o * o, axis=-1, keepdims=True) + EPS) * nw_ref[...]
        o_ref[0, pl.ds(s, br), :] = (o * _silu(z_ref[0, pl.ds(s, br), :].astype(F32))).astype(BF16)
        return carry

    lax.fori_loop(0, lp // br, fin, 0)


def _deltanet(qkv, z, logits, conv_w, par, norm_w, l_real):
    bsz, lp, _ = qkv.shape
    d, nh = DN_HEAD_DIM, DN_HEADS
    nc = lp // CHUNK
    group = next(g for g in (6, 4, 3, 2, 1) if nc % g == 0)
    head = lambda off: pl.BlockSpec((1, lp, d), lambda b, h: (b, 0, off + h))
    cw = lambda off: pl.BlockSpec((conv_w.shape[0], d), lambda b, h: (0, off + h))
    return pl.pallas_call(
        functools.partial(_dn_kernel, lp=lp, l_real=l_real, group=group),
        grid=(bsz, nh),
        in_specs=[head(0), head(nh), head(2 * nh), head(0),
                  pl.BlockSpec((1, lp, LANE), lambda b, h: (b, 0, 0)),
                  cw(0), cw(nh), cw(2 * nh),
                  pl.BlockSpec(par.shape, lambda b, h: (0, 0)),
                  pl.BlockSpec((1, d), lambda b, h: (0, 0))],
        out_specs=head(0),
        out_shape=jax.ShapeDtypeStruct((bsz, lp, nh * d), BF16),
        scratch_shapes=[pltpu.VMEM((lp, d), F32), pltpu.VMEM((lp, d), F32), pltpu.VMEM((lp, d), F32),
                        pltpu.VMEM((lp, LANE), F32),
                        pltpu.VMEM((2, lp, d), BF16), pltpu.VMEM((2, nc * d, d), BF16),
                        pltpu.VMEM((2, lp, d), F32), pltpu.VMEM((2, nc * d, d), F32),
                        pltpu.VMEM((2, nc * 8, LANE), F32)],
        compiler_params=_params(2),
        name="deltanet",
    )(qkv, qkv, qkv, z, logits, conv_w, conv_w, conv_w, par, norm_w)


def _merge_kernel(h_ref, y_ref, b_ref, ch_ref, chp_ref, chn_ref, dn_ref, gate_ref, sw_ref,
                  wf_ref, wsc_ref, wdn_ref, wout_ref, gpost_ref, o_ref, *, d_model):
    tm = h_ref.shape[0]
    ch = ch_ref[...].astype(F32)
    rows = lax.broadcasted_iota(jnp.int32, (tm, 1), 0)
    i = pl.program_id(0)
    halo_prev = jnp.where(i > 0, chp_ref[...].astype(F32)[15:16], 0.0)
    halo_next = jnp.where(i < pl.num_programs(0) - 1, chn_ref[...].astype(F32)[0:1], 0.0)
    ch_prev = jnp.where(rows == 0, halo_prev, pltpu.roll(ch, 1, 0))
    ch_next = jnp.where(rows == tm - 1, halo_next, pltpu.roll(ch, tm - 1, 0))
    sw = sw_ref[...]
    sc = b_ref[...].astype(F32) * (sw[0:1] * ch_prev + sw[1:2] * ch + sw[2:3] * ch_next)
    y_f = _dot(y_ref[...], wf_ref[...])
    y_sc = _dot(sc.astype(BF16), wsc_ref[...])
    y_dn = _dot(dn_ref[...], wdn_ref[...])
    merged = (gate_ref[:, 0:d_model].astype(F32) * y_f
              + gate_ref[:, d_model:2 * d_model].astype(F32) * y_sc
              + gate_ref[:, 2 * d_model:3 * d_model].astype(F32) * y_dn)
    out = _dot(merged.astype(BF16), wout_ref[...])
    o_ref[...] = h_ref[...] + _rms(out, gpost_ref[...])


def _merge(h2d, y, bsc, ch, dn, gate, sw, wf, wsc, wdn, wout, gpost, tm):
    t, d = h2d.shape
    row = lambda n: pl.BlockSpec((tm, n), lambda i: (i, 0))
    d_sc = ch.shape[1]
    hb = tm // 16
    last = t // 16 - 1
    return pl.pallas_call(
        functools.partial(_merge_kernel, d_model=d),
        grid=(t // tm,),
        in_specs=[row(d), row(y.shape[1]), row(d_sc), row(d_sc),
                  pl.BlockSpec((16, d_sc), lambda i: (jnp.maximum(i * hb - 1, 0), 0)),
                  pl.BlockSpec((16, d_sc), lambda i: (jnp.minimum((i + 1) * hb, last), 0)),
                  row(dn.shape[1]), row(gate.shape[1]),
                  _resident(sw.shape), _resident(wf.shape), _resident(wsc.shape), _resident(wdn.shape),
                  _resident(wout.shape), _resident((1, d))],
        out_specs=row(d),
        out_shape=jax.ShapeDtypeStruct((t, d), F32),
        compiler_params=_params(1),
        name="merge",
    )(h2d, y, bsc, ch, ch, ch, dn, gate, sw, wf, wsc, wdn, wout, gpost)


def _channel_dft_matrix(d_f):
    gd = FOURIER_GROUP_DIM
    idx = np.arange(gd)
    ang = 2.0 * np.pi * ((idx[:, None] * idx[None, :]) % gd) / gd
    blk = np.eye(d_f // gd)
    return np.concatenate([np.kron(blk, np.cos(ang)), np.kron(blk, -np.sin(ang))], axis=1)


def _pick_tile(n, candidates):
    return next(c for c in candidates if n % c == 0)


def kernel(x, meta_tokens, norm_gains, ffn1_wi, ffn1_wo, ffn2_wi, ffn2_wo, w_in, sconv_w, dn_conv_w,
           dn_A_log, dn_dt_bias, dn_norm, w_fourier, w_sconv_out, w_dn_out, w_out):
    bsz, seq, d = x.shape
    depth = norm_gains.shape[0]
    d_f = w_fourier.shape[1]
    d_sc = w_sconv_out.shape[1]
    d_dn = w_dn_out.shape[1]
    n_log = 4 * DN_HEADS
    l_real = N_META + seq
    lp = -(-(FRONT_PAD + l_real) // LANE) * LANE
    t = bsz * lp
    tm = _pick_tile(t, (1024, 768, 512, 384, 256, 128))
    tmf = _pick_tile(lp, (384, 256, 128))

    meta = jnp.broadcast_to(meta_tokens[None].astype(x.dtype), (bsz, N_META, d))
    h = jnp.concatenate([jnp.zeros((bsz, FRONT_PAD, d), x.dtype), meta, x,
                         jnp.zeros((bsz, lp - FRONT_PAD - l_real, d), x.dtype)], axis=1).reshape(t, d)

    wcs = jnp.asarray(_channel_dft_matrix(d_f), F32).astype(BF16)
    seeds = _dft_seeds(lp, l_real)
    scale = 1.0 / math.sqrt(l_real * FOURIER_GROUP_DIM)
    o_main = d_f + 3 * d_sc + 4 * d_dn

    for l in range(depth):
        gains = norm_gains[l][:, None, :]
        h = _ffn(h, gains[0], gains[1], ffn1_wi[l].astype(BF16), ffn1_wo[l].astype(BF16), tm)

        w = w_in[l]
        wlog = jnp.pad(w[:, o_main:o_main + n_log], ((0, 0), (0, LANE - n_log))).astype(BF16)
        ab, bsc, ch, qkv, z, logits, gate = _inproj(
            h, gains[2], w[:, :o_main].astype(BF16), wlog, w[:, o_main + n_log:].astype(BF16), wcs,
            tm, d_f, d_sc, d_dn)
        y = _dft(ab.reshape(bsz, lp, 2 * d_f), seeds, tmf, d_f, scale)
        par = jnp.zeros((8, LANE), F32)
        par = par.at[0, 2 * DN_HEADS:n_log].set(dn_A_log[l].reshape(-1))
        par = par.at[1, 2 * DN_HEADS:n_log].set(dn_dt_bias[l].reshape(-1))
        dn = _deltanet(qkv.reshape(bsz, lp, 3 * d_dn), z.reshape(bsz, lp, d_dn),
                       logits.reshape(bsz, lp, LANE), dn_conv_w[l], par, dn_norm[l][None, :], l_real)
        h = _merge(h, y.reshape(t, d_f), bsc, ch, dn.reshape(t, d_dn), gate, sconv_w[l],
                   w_fourier[l].astype(BF16), w_sconv_out[l].astype(BF16), w_dn_out[l].astype(BF16),
                   w_out[l].astype(BF16), gains[3], tm)

        h = _ffn(h, gains[4], gains[5], ffn2_wi[l].astype(BF16), ffn2_wo[l].astype(BF16), tm)

    return h.reshape(bsz, lp, d)[:, FRONT_PAD + N_META:FRONT_PAD + l_real]
```

```python
import functools
import math

import numpy as np
import jax
import jax.numpy as jnp
from jax import lax
from jax.experimental import pallas as pl
from jax.experimental.pallas import tpu as pltpu

EPS = 1e-6
N_META = 16
CHUNK = 64
FRONT_PAD = CHUNK - N_META
DN_HEADS = 4
DN_HEAD_DIM = 128
FOURIER_GROUP_DIM = 64
N_BRANCH = 3
LANE = 128
BF16 = jnp.bfloat16
F32 = jnp.float32
VMEM_LIMIT_BYTES = 58 * 1024 * 1024

_NT = (((1,), (1,)), ((), ()))
_TN = (((0,), (0,)), ((), ()))


def _dot(a, b):
    return jnp.dot(a, b, preferred_element_type=F32)


def _rms(x, g):
    ms = jnp.mean(x * x, axis=-1, keepdims=True)
    return x * lax.rsqrt(ms + EPS) * g


def _silu(x):
    return x * jax.nn.sigmoid(x)


def _softplus(x):
    return jnp.maximum(x, 0.0) + jnp.log1p(jnp.exp(-jnp.abs(x)))


def _conv3(x, prev_row, next_row, w):
    n = x.shape[0]
    rows = lax.broadcasted_iota(jnp.int32, (n, 1), 0)
    x_prev = jnp.where(rows == 0, prev_row, pltpu.roll(x, 1, 0))
    x_next = jnp.where(rows == n - 1, next_row, pltpu.roll(x, n - 1, 0))
    return w[0:1] * x_prev + w[1:2] * x + w[2:3] * x_next


def _params(n_axes):
    return pltpu.CompilerParams(dimension_semantics=("arbitrary",) * n_axes,
                                vmem_limit_bytes=VMEM_LIMIT_BYTES)


def _resident(shape):
    zeros = (0,) * len(shape)
    return pl.BlockSpec(shape, lambda *_: zeros, pipeline_mode=pl.Buffered(1))


def _ffn_kernel(h_ref, gpre_ref, gpost_ref, wi_ref, wo_ref, o_ref, hm_ref, *, d_ff, fch):
    x = h_ref[...]
    xn = _rms(x, gpre_ref[...]).astype(BF16)
    for c in range(d_ff // fch):
        a = _dot(xn, wi_ref[:, c * fch:(c + 1) * fch])
        b = _dot(xn, wi_ref[:, d_ff + c * fch:d_ff + (c + 1) * fch])
        hm_ref[:, c * fch:(c + 1) * fch] = (_silu(a) * b).astype(BF16)
    y = _dot(hm_ref[...], wo_ref[...])
    o_ref[...] = x + 0.5 * _rms(y, gpost_ref[...])


def _ffn(h2d, gpre, gpost, wi, wo, tm):
    t, d = h2d.shape
    d_ff = wo.shape[0]
    fch = d_ff // 2 if (d_ff // 2) % LANE == 0 else d_ff
    return pl.pallas_call(
        functools.partial(_ffn_kernel, d_ff=d_ff, fch=fch),
        grid=(t // tm,),
        in_specs=[pl.BlockSpec((tm, d), lambda i: (i, 0)),
                  _resident((1, d)), _resident((1, d)),
                  _resident((d, 2 * d_ff)), _resident((d_ff, d))],
        out_specs=pl.BlockSpec((tm, d), lambda i: (i, 0)),
        out_shape=jax.ShapeDtypeStruct((t, d), F32),
        scratch_shapes=[pltpu.VMEM((tm, d_ff), BF16)],
        compiler_params=_params(1),
        name="ffn",
    )(h2d, gpre, gpost, wi, wo)


def _inproj_kernel(h_ref, hp_ref, hn_ref, g_ref, wmain_ref, wlog_ref, wgate_ref, wcs_ref, scw_ref, cw_ref,
                   par_ref, ab_ref, sc_ref, qkv_ref, z_ref, dnc_ref, gate_ref,
                   *, d_f, d_sc, d_dn, lp, l_real):
    i = pl.program_id(0)
    tm = h_ref.shape[0]
    d, nh = DN_HEAD_DIM, DN_HEADS
    gain = g_ref[...]
    xn = _rms(h_ref[...], gain).astype(BF16)
    o_sc, o_qkv, o_z = d_f, d_f + 3 * d_sc, d_f + 3 * d_sc + 3 * d_dn

    xh = _rms(jnp.concatenate([hp_ref[...], hn_ref[...]], axis=0), gain).astype(BF16)
    ph = _dot(xh, wmain_ref[:, o_sc:o_z])
    prev = jnp.where(i > 0, ph[7:8], 0.0)
    nxt = jnp.where(i < pl.num_programs(0) - 1, ph[8:9], 0.0)

    uf = _dot(xn, wmain_ref[:, 0:d_f])
    ab_ref[...] = _dot(uf.astype(BF16), wcs_ref[...]).astype(BF16)

    usc = _dot(xn, wmain_ref[:, o_sc:o_qkv])
    ch = usc[:, d_sc:2 * d_sc] * usc[:, 2 * d_sc:]
    ch_prev = prev[:, d_sc:2 * d_sc] * prev[:, 2 * d_sc:3 * d_sc]
    ch_next = nxt[:, d_sc:2 * d_sc] * nxt[:, 2 * d_sc:3 * d_sc]
    sc_ref[...] = (usc[:, :d_sc] * _conv3(ch, ch_prev, ch_next, scw_ref[...])).astype(BF16)

    qkv = _dot(xn, wmain_ref[:, o_qkv:o_z])
    y = _silu(_conv3(qkv, prev[:, 3 * d_sc:], nxt[:, 3 * d_sc:], cw_ref[...]))
    for blk in range(3 * nh):
        yb = y[:, blk * d:(blk + 1) * d]
        if blk < 2 * nh:
            yb = yb * lax.rsqrt(jnp.sum(yb * yb, axis=-1, keepdims=True) + EPS)
        if blk < nh:
            yb = yb * (d ** -0.5)
        qkv_ref[:, blk * d:(blk + 1) * d] = yb.astype(BF16)

    z_ref[...] = _dot(xn, wmain_ref[:, o_z:o_z + d_dn]).astype(BF16)
    gate_ref[...] = jax.nn.sigmoid(_dot(xn, wgate_ref[...])).astype(BF16)

    lg = _dot(xn, wlog_ref[...])
    rows = lax.broadcasted_iota(jnp.int32, (tm, 1), 0)
    lane = lax.broadcasted_iota(jnp.int32, (1, LANE), 1)
    pos = i * tm + rows - ((i * tm) // lp) * lp
    pos = jnp.where(pos >= lp, pos - lp, pos)
    beta = jax.nn.sigmoid(lg)
    g = -jnp.exp(par_ref[0:1, :]) * _softplus(lg + par_ref[1:2, :])
    bg = jnp.where(lane < 2 * nh, beta, g)
    bg = jnp.where(pos >= FRONT_PAD, bg, 0.0)
    bg = jnp.where(pos < FRONT_PAD + l_real, bg, 0.0)
    rc = rows & (CHUNK - 1)
    pre = bg
    suf = bg
    s = 1
    while s < CHUNK:
        pre = pre + jnp.where(rc >= s, pltpu.roll(pre, s, 0), 0.0)
        suf = suf + jnp.where(rc < CHUNK - s, pltpu.roll(suf, tm - s, 0), 0.0)
        s *= 2
    dnc_ref[...] = jnp.where(lane < 2 * nh, bg, jnp.where(lane < 3 * nh, pre, suf))


def _inproj(h2d, g, wmain, wlog, wgate, wcs, scw, cw, par, tm, d_f, d_sc, d_dn, lp, l_real):
    t, d = h2d.shape
    row = lambda n: pl.BlockSpec((tm, n), lambda i: (i, 0))
    hb = tm // 8
    last = t // 8 - 1
    outs = [(2 * d_f, BF16), (d_sc, BF16), (3 * d_dn, BF16), (d_dn, BF16), (LANE, F32),
            (wgate.shape[1], BF16)]
    return pl.pallas_call(
        functools.partial(_inproj_kernel, d_f=d_f, d_sc=d_sc, d_dn=d_dn, lp=lp, l_real=l_real),
        grid=(t // tm,),
        in_specs=[row(d),
                  pl.BlockSpec((8, d), lambda i: (jnp.maximum(i * hb - 1, 0), 0)),
                  pl.BlockSpec((8, d), lambda i: (jnp.minimum((i + 1) * hb, last), 0)),
                  _resident((1, d)), _resident(wmain.shape), _resident(wlog.shape),
                  _resident(wgate.shape), _resident(wcs.shape), _resident(scw.shape), _resident(cw.shape),
                  _resident(par.shape)],
        out_specs=[row(n) for n, _ in outs],
        out_shape=[jax.ShapeDtypeStruct((t, n), dt) for n, dt in outs],
        compiler_params=_params(1),
        name="inproj",
    )(h2d, h2d, h2d, g, wmain, wlog, wgate, wcs, scw, cw, par)


def _dft_kernel(pc_ref, ps_ref, qc_ref, qs_ref, a_ref, b_ref, y_ref, c_scr, s_scr, *, scale):
    @pl.when(pl.program_id(1) == 0)
    def _():
        pc = pc_ref[...]
        ps = ps_ref[...]
        for nb in range(c_scr.shape[1] // LANE):
            qc = qc_ref[:, nb:nb + 1]
            qs = qs_ref[:, nb:nb + 1]
            c_scr[:, nb * LANE:(nb + 1) * LANE] = (qc * pc - qs * ps).astype(BF16)
            s_scr[:, nb * LANE:(nb + 1) * LANE] = (qs * pc + qc * ps).astype(BF16)

    y = _dot(c_scr[...], a_ref[0]) + _dot(s_scr[...], b_ref[0])
    y_ref[0] = (y * scale).astype(BF16)


def _dft_seeds(lp, l_real):
    k = jnp.arange(lp, dtype=jnp.int32) - FRONT_PAD
    valid = (k >= 0) & (k < l_real)
    kk = jnp.where(valid, k, 0)[:, None]
    theta = 2.0 * math.pi / l_real
    j = jnp.arange(LANE, dtype=jnp.int32)[None, :]
    rp = ((kk * j) % l_real).astype(F32) * theta
    nb = lp // LANE
    ob = (jnp.arange(LANE, dtype=jnp.int32) * LANE - FRONT_PAD) % l_real
    rq = ((kk * ob[None, :]) % l_real).astype(F32) * theta
    qmask = valid[:, None] & (jnp.arange(LANE)[None, :] < nb)
    return (jnp.cos(rp), jnp.sin(rp),
            jnp.where(qmask, jnp.cos(rq), 0.0), jnp.where(qmask, jnp.sin(rq), 0.0))


def _dft(ab, seeds, tmf, d_f, scale):
    bsz, lp, _ = ab.shape
    pc, ps, qc, qs = seeds
    seed = pl.BlockSpec((tmf, LANE), lambda m, b: (m, 0))
    return pl.pallas_call(
        functools.partial(_dft_kernel, scale=scale),
        grid=(lp // tmf, bsz),
        in_specs=[seed, seed, seed, seed,
                  pl.BlockSpec((1, lp, d_f), lambda m, b: (b, 0, 0)),
                  pl.BlockSpec((1, lp, d_f), lambda m, b: (b, 0, 1))],
        out_specs=pl.BlockSpec((1, tmf, d_f), lambda m, b: (b, m, 0)),
        out_shape=jax.ShapeDtypeStruct((bsz, lp, d_f), BF16),
        scratch_shapes=[pltpu.VMEM((tmf, lp), BF16), pltpu.VMEM((tmf, lp), BF16)],
        compiler_params=_params(2),
        name="dft",
    )(pc, ps, qc, qs, ab, ab)


def _dn_kernel(q_ref, k_ref, v_ref, z_ref, dnc_ref, r_ref, nw_ref, o_ref, acc, st, *, lp, pairs):
    c_sz, d, nh = CHUNK, DN_HEAD_DIM, DN_HEADS
    nc = lp // c_sz
    br = LANE

    ii = lax.broadcasted_iota(jnp.int32, (c_sz, 2 * c_sz), 0)
    jl = lax.broadcasted_iota(jnp.int32, (c_sz, 2 * c_sz), 1)
    left = jl < c_sz
    jj = jnp.where(left, jl, jl - c_sz)
    causal = jnp.where(left, ii - jj, jj - ii)
    tri = causal >= 0
    stri = causal > 0
    eye2 = (ii == jj).astype(F32)
    zeros_c = jnp.zeros((c_sz, 2 * d), BF16)
    zeros_d = jnp.zeros((d, d), BF16)

    def pair(x_f, x_b):
        return jnp.where(left, x_f, x_b)

    def blockdiag(x2):
        return jnp.concatenate([jnp.where(left, x2, 0.0), jnp.where(left, 0.0, x2)], axis=0).astype(BF16)

    def blockrows(x4):
        return jnp.concatenate([jnp.concatenate([x4[:, :2 * d], zeros_c], axis=1),
                                jnp.concatenate([zeros_c, x4[:, 2 * d:]], axis=1)], axis=0)

    acc[...] = jnp.zeros(acc.shape, F32)
    st[...] = jnp.zeros(st.shape, F32)

    def super_iter(it, carry):
        cfs = [it * pairs + p for p in range(pairs)]
        cbs = [nc - 1 - c for c in cfs]
        rf = [pl.multiple_of(c * c_sz, c_sz) for c in cfs]
        rb = [pl.multiple_of(c * c_sz, c_sz) for c in cbs]
        items = [(p, h) for p in range(pairs) for h in range(nh)]
        hs = lambda h: slice(h * d, (h + 1) * d)
        qrow = [(q_ref[0, pl.ds(rf[p], c_sz), :], q_ref[0, pl.ds(rb[p], c_sz), :]) for p in range(pairs)]
        krow = [(k_ref[0, pl.ds(rf[p], c_sz), :], k_ref[0, pl.ds(rb[p], c_sz), :]) for p in range(pairs)]
        vrow = [(v_ref[0, pl.ds(rf[p], c_sz), :], v_ref[0, pl.ds(rb[p], c_sz), :]) for p in range(pairs)]
        drow = [(dnc_ref[0, pl.ds(rf[p], c_sz), :], dnc_ref[0, pl.ds(rb[p], c_sz), :]) for p in range(pairs)]
        qf = [qrow[p][0][:, hs(h)] for p, h in items]
        qb = [qrow[p][1][:, hs(h)] for p, h in items]
        kf = [krow[p][0][:, hs(h)] for p, h in items]
        kb = [krow[p][1][:, hs(h)] for p, h in items]
        vf = [vrow[p][0][:, hs(h)].astype(F32) for p, h in items]
        vb = [vrow[p][1][:, hs(h)].astype(F32) for p, h in items]
        btf = [drow[p][0][:, h:h + 1] for p, h in items]
        btb = [drow[p][1][:, nh + h:nh + h + 1] for p, h in items]
        gcf = [drow[p][0][:, 2 * nh + h:2 * nh + h + 1] for p, h in items]
        gcb = [drow[p][1][:, 3 * nh + h:3 * nh + h + 1] for p, h in items]
        grow = [pair(r_ref[0, h, pl.ds(cfs[p], 1), :], r_ref[0, h, pl.ds(cbs[p], 1), :]) for p, h in items]
        dec = [jnp.where(tri, jnp.exp(jnp.minimum(pair(f, b) - g, 0.0)), 0.0)
               for f, b, g in zip(gcf, gcb, grow)]
        gram = [lax.dot_general(
            jnp.concatenate([jnp.concatenate([a, b], axis=1), jnp.concatenate([c, e], axis=1)], axis=0),
            jnp.concatenate([jnp.concatenate([c, zeros_c[:, :d]], axis=1),
                             jnp.concatenate([zeros_c[:, :d], e], axis=1)], axis=0),
            _NT, preferred_element_type=F32) for a, b, c, e in zip(qf, qb, kf, kb)]
        nm = [jnp.where(stri, -(pair(f, b) * g[c_sz:] * dc), 0.0)
              for f, b, g, dc in zip(btf, btb, gram, dec)]
        sacc = [eye2 + x for x in nm]
        qpow = [_dot(x.astype(BF16), blockdiag(x)) for x in nm]
        for _ in range(4):
            r = [_dot(jnp.concatenate([s, q], axis=0).astype(BF16), blockdiag(q)) for s, q in zip(sacc, qpow)]
            sacc = [s + x[:c_sz] for s, x in zip(sacc, r)]
            qpow = [x[c_sz:] for x in r]
        tinv = [s + _dot(s.astype(BF16), blockdiag(q)) for s, q in zip(sacc, qpow)]
        egf = [jnp.exp(x) for x in gcf]
        egb = [jnp.exp(x) for x in gcb]
        kf32 = [x.astype(F32) for x in kf]
        kb32 = [x.astype(F32) for x in kb]
        rhs = [blockrows(jnp.concatenate([v1 * bf, k1 * (bf * ef), v2 * bb, k2 * (bb * eb)], axis=1).astype(BF16))
               for v1, k1, v2, k2, bf, bb, ef, eb in zip(vf, kf32, vb, kb32, btf, btb, egf, egb)]
        uw = [blockrows(_dot(t.astype(BF16), x).astype(BF16)) for t, x in zip(tinv, rhs)]
        x1 = [_dot((g[:c_sz] * dc).astype(BF16), x) for g, dc, x in zip(gram, dec, uw)]
        gtf = [x[c_sz - 1:c_sz] for x in gcf]
        gtb = [x[0:1] for x in gcb]
        kd = [jnp.concatenate([k1 * jnp.exp(tf - f), k2 * jnp.exp(tb - b)], axis=0).astype(BF16)
              for k1, k2, tf, tb, f, b in zip(kf32, kb32, gtf, gtb, gcf, gcb)]
        x2 = [lax.dot_general(a, x, _TN, preferred_element_type=F32) for a, x in zip(kd, uw)]
        sel = lambda x, o: jnp.concatenate([x[:, o:o + d], x[:, 2 * d + o:3 * d + o]], axis=1)
        lhs = [jnp.concatenate([
            (jnp.concatenate([a.astype(F32) * ef, b.astype(F32) * eb], axis=1) - sel(y1, d)).astype(BF16),
            (-sel(y2, d)).astype(BF16)], axis=0)
            for a, b, ef, eb, y1, y2 in zip(qf, qb, egf, egb, x1, x2)]
        bo = [sel(y1, 0) for y1 in x1]
        bs = [sel(y2, 0) for y2 in x2]
        gl = [jnp.concatenate([jnp.broadcast_to(jnp.exp(tf), (1, d)), jnp.broadcast_to(jnp.exp(tb), (1, d))],
                              axis=1) for tf, tb in zip(gtf, gtb)]

        state = [st[h] for h in range(nh)]
        for n, (p, h) in enumerate(items):
            s2 = state[h]
            sbd = jnp.concatenate([jnp.concatenate([s2[:, :d].astype(BF16), zeros_d], axis=1),
                                   jnp.concatenate([zeros_d, s2[:, d:].astype(BF16)], axis=1)], axis=0)
            r = _dot(lhs[n], sbd)
            o2 = r[:c_sz] + bo[n]
            state[h] = gl[n] * s2 + r[c_sz:] + bs[n]
            acc[pl.ds(rf[p], c_sz), hs(h)] += o2[:, :d]
            acc[pl.ds(rb[p], c_sz), hs(h)] += o2[:, d:]
        for h in range(nh):
            st[h] = state[h]
        return carry

    lax.fori_loop(0, nc // pairs, super_iter, 0)

    def fin(i, carry):
        s = pl.multiple_of(i * br, br)
        zz = z_ref[0, pl.ds(s, br), :].astype(F32)
        for h in range(nh):
            o = acc[pl.ds(s, br), h * d:(h + 1) * d]
            o = o * lax.rsqrt(jnp.mean(o * o, axis=-1, keepdims=True) + EPS) * nw_ref[...]
            o_ref[0, pl.ds(s, br), h * d:(h + 1) * d] = (o * _silu(zz[:, h * d:(h + 1) * d])).astype(BF16)
        return carry

    lax.fori_loop(0, lp // br, fin, 0)


def _deltanet(qkv, z, dnc, rrow, norm_w):
    bsz, lp, _ = qkv.shape
    d, nh = DN_HEAD_DIM, DN_HEADS
    nc = lp // CHUNK
    pairs = next(g for g in (3, 2, 1) if nc % g == 0)
    once = lambda blk, idx: pl.BlockSpec(blk, idx, pipeline_mode=pl.Buffered(1))
    return pl.pallas_call(
        functools.partial(_dn_kernel, lp=lp, pairs=pairs),
        grid=(bsz,),
        in_specs=[once((1, lp, nh * d), lambda b: (b, 0, 0)),
                  once((1, lp, nh * d), lambda b: (b, 0, 1)),
                  once((1, lp, nh * d), lambda b: (b, 0, 2)),
                  once((1, lp, nh * d), lambda b: (b, 0, 0)),
                  pl.BlockSpec((1, lp, LANE), lambda b: (b, 0, 0)),
                  pl.BlockSpec((1, nh, nc, 2 * CHUNK), lambda b: (b, 0, 0, 0)),
                  _resident((1, d))],
        out_specs=pl.BlockSpec((1, lp, nh * d), lambda b: (b, 0, 0)),
        out_shape=jax.ShapeDtypeStruct((bsz, lp, nh * d), BF16),
        scratch_shapes=[pltpu.VMEM((lp, nh * d), F32), pltpu.VMEM((nh, d, 2 * d), F32)],
        compiler_params=_params(1),
        name="deltanet",
    )(qkv, qkv, qkv, z, dnc, rrow, norm_w)


def _merge_kernel(h_ref, y_ref, sc_ref, dn_ref, gate_ref, wf_ref, wsc_ref, wdn_ref, wout_ref, gpost_ref,
                  o_ref, *, d_model):
    y_f = _dot(y_ref[...], wf_ref[...])
    y_sc = _dot(sc_ref[...], wsc_ref[...])
    y_dn = _dot(dn_ref[...], wdn_ref[...])
    merged = (gate_ref[:, 0:d_model].astype(F32) * y_f
              + gate_ref[:, d_model:2 * d_model].astype(F32) * y_sc
              + gate_ref[:, 2 * d_model:3 * d_model].astype(F32) * y_dn)
    out = _dot(merged.astype(BF16), wout_ref[...])
    o_ref[...] = h_ref[...] + _rms(out, gpost_ref[...])


def _merge(h2d, y, sc, dn, gate, wf, wsc, wdn, wout, gpost, tm):
    t, d = h2d.shape
    row = lambda n: pl.BlockSpec((tm, n), lambda i: (i, 0))
    return pl.pallas_call(
        functools.partial(_merge_kernel, d_model=d),
        grid=(t // tm,),
        in_specs=[row(d), row(y.shape[1]), row(sc.shape[1]), row(dn.shape[1]), row(gate.shape[1]),
                  _resident(wf.shape), _resident(wsc.shape), _resident(wdn.shape),
                  _resident(wout.shape), _resident((1, d))],
        out_specs=row(d),
        out_shape=jax.ShapeDtypeStruct((t, d), F32),
        compiler_params=_params(1),
        name="merge",
    )(h2d, y, sc, dn, gate, wf, wsc, wdn, wout, gpost)


def _channel_dft_matrix(d_f):
    gd = FOURIER_GROUP_DIM
    idx = np.arange(gd)
    ang = 2.0 * np.pi * ((idx[:, None] * idx[None, :]) % gd) / gd
    blk = np.eye(d_f // gd)
    return np.concatenate([np.kron(blk, np.cos(ang)), np.kron(blk, -np.sin(ang))], axis=1)


def _pick_tile(n, candidates):
    return next(c for c in candidates if n % c == 0)


def kernel(x, meta_tokens, norm_gains, ffn1_wi, ffn1_wo, ffn2_wi, ffn2_wo, w_in, sconv_w, dn_conv_w,
           dn_A_log, dn_dt_bias, dn_norm, w_fourier, w_sconv_out, w_dn_out, w_out):
    bsz, seq, d = x.shape
    depth = norm_gains.shape[0]
    d_f = w_fourier.shape[1]
    d_sc = w_sconv_out.shape[1]
    d_dn = w_dn_out.shape[1]
    nh = DN_HEADS
    n_log = 4 * nh
    l_real = N_META + seq
    lp = -(-(FRONT_PAD + l_real) // LANE) * LANE
    nc = lp // CHUNK
    t = bsz * lp
    tm = _pick_tile(t, (1024, 768, 512, 384, 256, 128))
    tmi = _pick_tile(t, (512, 384, 256, 128))
    tmf = _pick_tile(lp, (384, 256, 128))

    meta = jnp.broadcast_to(meta_tokens[None].astype(x.dtype), (bsz, N_META, d))
    h = jnp.concatenate([jnp.zeros((bsz, FRONT_PAD, d), x.dtype), meta, x,
                         jnp.zeros((bsz, lp - FRONT_PAD - l_real, d), x.dtype)], axis=1).reshape(t, d)

    wcs = jnp.asarray(_channel_dft_matrix(d_f), F32).astype(BF16)
    seeds = _dft_seeds(lp, l_real)
    scale = 1.0 / math.sqrt(l_real * FOURIER_GROUP_DIM)
    o_main = d_f + 3 * d_sc + 4 * d_dn

    for l in range(depth):
        gains = norm_gains[l][:, None, :]
        h = _ffn(h, gains[0], gains[1], ffn1_wi[l].astype(BF16), ffn1_wo[l].astype(BF16), tm)

        w = w_in[l]
        wlog = jnp.pad(w[:, o_main:o_main + n_log], ((0, 0), (0, LANE - n_log))).astype(BF16)
        par = jnp.zeros((8, LANE), F32)
        par = par.at[0, 2 * nh:n_log].set(dn_A_log[l].reshape(-1))
        par = par.at[1, 2 * nh:n_log].set(dn_dt_bias[l].reshape(-1))
        ab, sc, qkv, z, dnc, gate = _inproj(
            h, gains[2], w[:, :o_main].astype(BF16), wlog, w[:, o_main + n_log:].astype(BF16), wcs,
            sconv_w[l], dn_conv_w[l], par, tmi, d_f, d_sc, d_dn, lp, l_real)
        y = _dft(ab.reshape(bsz, lp, 2 * d_f), seeds, tmf, d_f, scale)
        r = jnp.transpose(dnc.reshape(bsz, nc, CHUNK, LANE)[..., 2 * nh:n_log], (0, 3, 1, 2))
        rrow = jnp.concatenate([r[:, :nh], r[:, nh:]], axis=-1)
        dn = _deltanet(qkv.reshape(bsz, lp, 3 * d_dn), z.reshape(bsz, lp, d_dn),
                       dnc.reshape(bsz, lp, LANE), rrow, dn_norm[l][None, :])
        h = _merge(h, y.reshape(t, d_f), sc, dn.reshape(t, d_dn), gate,
                   w_fourier[l].astype(BF16), w_sconv_out[l].astype(BF16), w_dn_out[l].astype(BF16),
                   w_out[l].astype(BF16), gains[3], tm)

        h = _ffn(h, gains[4], gains[5], ffn2_wi[l].astype(BF16), ffn2_wo[l].astype(BF16), tm)

    return h.reshape(bsz, lp, d)[:, FRONT_PAD + N_META:FRONT_PAD + l_real]
```

```python
import functools
import math

import numpy as np
import jax
import jax.numpy as jnp
from jax import lax
from jax.experimental import pallas as pl
from jax.experimental.pallas import tpu as pltpu

EPS = 1e-6
N_META = 16
CHUNK = 64
FRONT_PAD = CHUNK - N_META
DN_HEADS = 4
DN_HEAD_DIM = 128
FOURIER_GROUP_DIM = 64
N_BRANCH = 3
LANE = 128
BF16 = jnp.bfloat16
F32 = jnp.float32
VMEM_LIMIT_BYTES = 58 * 1024 * 1024

_NT = (((1,), (1,)), ((), ()))
_TN = (((0,), (0,)), ((), ()))


def _dot(a, b):
    return jnp.dot(a, b, preferred_element_type=F32)


def _rms(x, g):
    ms = jnp.mean(x * x, axis=-1, keepdims=True)
    return x * lax.rsqrt(ms + EPS) * g


def _silu(x):
    return x * jax.nn.sigmoid(x)


def _softplus(x):
    return jnp.maximum(x, 0.0) + jnp.log1p(jnp.exp(-jnp.abs(x)))


def _conv3(x, prev_row, next_row, w):
    n = x.shape[0]
    rows = lax.broadcasted_iota(jnp.int32, (n, 1), 0)
    x_prev = jnp.where(rows == 0, prev_row, pltpu.roll(x, 1, 0))
    x_next = jnp.where(rows == n - 1, next_row, pltpu.roll(x, n - 1, 0))
    return w[0:1] * x_prev + w[1:2] * x + w[2:3] * x_next


def _params(n_axes):
    return pltpu.CompilerParams(dimension_semantics=("arbitrary",) * n_axes,
                                vmem_limit_bytes=VMEM_LIMIT_BYTES)


def _resident(shape):
    zeros = (0,) * len(shape)
    return pl.BlockSpec(shape, lambda *_: zeros, pipeline_mode=pl.Buffered(1))


def _ffn_kernel(h_ref, gpre_ref, gpost_ref, wi_ref, wo_ref, o_ref, hm_ref, *, d_ff, fch):
    x = h_ref[...]
    xn = _rms(x, gpre_ref[...]).astype(BF16)
    for c in range(d_ff // fch):
        a = _dot(xn, wi_ref[:, c * fch:(c + 1) * fch])
        b = _dot(xn, wi_ref[:, d_ff + c * fch:d_ff + (c + 1) * fch])
        hm_ref[:, c * fch:(c + 1) * fch] = (_silu(a) * b).astype(BF16)
    y = _dot(hm_ref[...], wo_ref[...])
    o_ref[...] = x + 0.5 * _rms(y, gpost_ref[...])


def _ffn(h2d, gpre, gpost, wi, wo, tm):
    t, d = h2d.shape
    d_ff = wo.shape[0]
    fch = d_ff // 2 if (d_ff // 2) % LANE == 0 else d_ff
    return pl.pallas_call(
        functools.partial(_ffn_kernel, d_ff=d_ff, fch=fch),
        grid=(t // tm,),
        in_specs=[pl.BlockSpec((tm, d), lambda i: (i, 0)),
                  _resident((1, d)), _resident((1, d)),
                  _resident((d, 2 * d_ff)), _resident((d_ff, d))],
        out_specs=pl.BlockSpec((tm, d), lambda i: (i, 0)),
        out_shape=jax.ShapeDtypeStruct((t, d), F32),
        scratch_shapes=[pltpu.VMEM((tm, d_ff), BF16)],
        compiler_params=_params(1),
        name="ffn",
    )(h2d, gpre, gpost, wi, wo)


def _inproj_kernel(h_ref, hp_ref, hn_ref, g_ref, wmain_ref, wlog_ref, wgate_ref, wcs_ref, scw_ref, cw_ref,
                   par_ref, ab_ref, sc_ref, qkv_ref, z_ref, dnc_ref, gate_ref,
                   *, d_f, d_sc, d_dn, lp, l_real):
    i = pl.program_id(0)
    tm = h_ref.shape[0]
    d, nh = DN_HEAD_DIM, DN_HEADS
    gain = g_ref[...]
    xn = _rms(h_ref[...], gain).astype(BF16)
    o_sc, o_qkv, o_z = d_f, d_f + 3 * d_sc, d_f + 3 * d_sc + 3 * d_dn

    xh = _rms(jnp.concatenate([hp_ref[...], hn_ref[...]], axis=0), gain).astype(BF16)
    ph = _dot(xh, wmain_ref[:, o_sc:o_z])
    prev = jnp.where(i > 0, ph[7:8], 0.0)
    nxt = jnp.where(i < pl.num_programs(0) - 1, ph[8:9], 0.0)

    uf = _dot(xn, wmain_ref[:, 0:d_f])
    ab_ref[...] = _dot(uf.astype(BF16), wcs_ref[...]).astype(BF16)

    usc = _dot(xn, wmain_ref[:, o_sc:o_qkv])
    ch = usc[:, d_sc:2 * d_sc] * usc[:, 2 * d_sc:]
    ch_prev = prev[:, d_sc:2 * d_sc] * prev[:, 2 * d_sc:3 * d_sc]
    ch_next = nxt[:, d_sc:2 * d_sc] * nxt[:, 2 * d_sc:3 * d_sc]
    sc_ref[...] = (usc[:, :d_sc] * _conv3(ch, ch_prev, ch_next, scw_ref[...])).astype(BF16)

    qkv = _dot(xn, wmain_ref[:, o_qkv:o_z])
    y = _silu(_conv3(qkv, prev[:, 3 * d_sc:], nxt[:, 3 * d_sc:], cw_ref[...]))
    for blk in range(3 * nh):
        yb = y[:, blk * d:(blk + 1) * d]
        if blk < 2 * nh:
            yb = yb * lax.rsqrt(jnp.sum(yb * yb, axis=-1, keepdims=True) + EPS)
        if blk < nh:
            yb = yb * (d ** -0.5)
        qkv_ref[:, blk * d:(blk + 1) * d] = yb.astype(BF16)

    z_ref[...] = _dot(xn, wmain_ref[:, o_z:o_z + d_dn]).astype(BF16)
    gate_ref[...] = jax.nn.sigmoid(_dot(xn, wgate_ref[...])).astype(BF16)

    lg = _dot(xn, wlog_ref[...])
    rows = lax.broadcasted_iota(jnp.int32, (tm, 1), 0)
    lane = lax.broadcasted_iota(jnp.int32, (1, LANE), 1)
    pos = i * tm + rows - ((i * tm) // lp) * lp
    pos = jnp.where(pos >= lp, pos - lp, pos)
    beta = jax.nn.sigmoid(lg)
    g = -jnp.exp(par_ref[0:1, :]) * _softplus(lg + par_ref[1:2, :])
    bg = jnp.where(lane < 2 * nh, beta, g)
    bg = jnp.where(pos >= FRONT_PAD, bg, 0.0)
    bg = jnp.where(pos < FRONT_PAD + l_real, bg, 0.0)
    rc = rows & (CHUNK - 1)
    pre = bg
    suf = bg
    s = 1
    while s < CHUNK:
        pre = pre + jnp.where(rc >= s, pltpu.roll(pre, s, 0), 0.0)
        suf = suf + jnp.where(rc < CHUNK - s, pltpu.roll(suf, tm - s, 0), 0.0)
        s *= 2
    dnc_ref[...] = jnp.where(lane < 2 * nh, bg, jnp.where(lane < 3 * nh, pre, suf))


def _inproj(h2d, g, wmain, wlog, wgate, wcs, scw, cw, par, tm, d_f, d_sc, d_dn, lp, l_real):
    t, d = h2d.shape
    row = lambda n: pl.BlockSpec((tm, n), lambda i: (i, 0))
    hb = tm // 8
    last = t // 8 - 1
    outs = [(2 * d_f, BF16), (d_sc, BF16), (3 * d_dn, BF16), (d_dn, BF16), (LANE, F32),
            (wgate.shape[1], BF16)]
    return pl.pallas_call(
        functools.partial(_inproj_kernel, d_f=d_f, d_sc=d_sc, d_dn=d_dn, lp=lp, l_real=l_real),
        grid=(t // tm,),
        in_specs=[row(d),
                  pl.BlockSpec((8, d), lambda i: (jnp.maximum(i * hb - 1, 0), 0)),
                  pl.BlockSpec((8, d), lambda i: (jnp.minimum((i + 1) * hb, last), 0)),
                  _resident((1, d)), _resident(wmain.shape), _resident(wlog.shape),
                  _resident(wgate.shape), _resident(wcs.shape), _resident(scw.shape), _resident(cw.shape),
                  _resident(par.shape)],
        out_specs=[row(n) for n, _ in outs],
        out_shape=[jax.ShapeDtypeStruct((t, n), dt) for n, dt in outs],
        compiler_params=_params(1),
        name="inproj",
    )(h2d, h2d, h2d, g, wmain, wlog, wgate, wcs, scw, cw, par)


def _dft_table_kernel(pc_ref, ps_ref, qc_ref, qs_ref, c_ref, s_ref):
    pc = pc_ref[...]
    ps = ps_ref[...]
    for nb in range(c_ref.shape[1] // LANE):
        qc = qc_ref[:, nb:nb + 1]
        qs = qs_ref[:, nb:nb + 1]
        c_ref[:, nb * LANE:(nb + 1) * LANE] = (qc * pc - qs * ps).astype(BF16)
        s_ref[:, nb * LANE:(nb + 1) * LANE] = (qs * pc + qc * ps).astype(BF16)


def _dft_tables(kp, l_real):
    k = jnp.arange(kp, dtype=jnp.int32) - FRONT_PAD
    valid = (k >= 0) & (k <= l_real // 2)
    kk = jnp.where(valid, k, 0)[:, None]
    theta = 2.0 * math.pi / l_real
    j = jnp.arange(LANE, dtype=jnp.int32)[None, :]
    rp = ((kk * j) % l_real).astype(F32) * theta
    ob = (jnp.arange(LANE, dtype=jnp.int32) * LANE - FRONT_PAD) % l_real
    rq = ((kk * ob[None, :]) % l_real).astype(F32) * theta
    qmask = valid[:, None] & (jnp.arange(LANE)[None, :] < kp // LANE)
    seeds = (jnp.cos(rp), jnp.sin(rp), jnp.where(qmask, jnp.cos(rq), 0.0), jnp.where(qmask, jnp.sin(rq), 0.0))
    tr = LANE
    seed = pl.BlockSpec((tr, LANE), lambda m: (m, 0))
    tab = pl.BlockSpec((tr, kp), lambda m: (m, 0))
    return pl.pallas_call(
        _dft_table_kernel,
        grid=(kp // tr,),
        in_specs=[seed] * 4,
        out_specs=[tab, tab],
        out_shape=[jax.ShapeDtypeStruct((kp, kp), BF16)] * 2,
        compiler_params=_params(1),
        name="dft_tables",
    )(*seeds)


def _dft_kernel(c_ref, s_ref, a_ref, b_ref, am_ref, bm_ref, yt_ref, yb_ref, *, scale, n_rows):
    keep = lax.broadcasted_iota(jnp.int32, (a_ref.shape[1], 1), 0) < n_rows
    ae = jnp.where(keep, a_ref[0].astype(F32) + am_ref[0].astype(F32), 0.0).astype(BF16)
    bo = jnp.where(keep, b_ref[0].astype(F32) - bm_ref[0].astype(F32), 0.0).astype(BF16)
    p = _dot(c_ref[...], ae)
    q = _dot(s_ref[...], bo)
    yt_ref[0] = ((p + q) * scale).astype(BF16)
    yb_ref[0] = ((p - q) * scale).astype(BF16)


def _dft(ab, tables, l_real, d_f, scale):
    bsz, lp, _ = ab.shape
    ctab, stab = tables
    kp = ctab.shape[0]
    hl = l_real // 2
    lo, hi = FRONT_PAD, FRONT_PAD + l_real
    mirrored = jnp.flip(ab[:, lo + hl + 1:hi], axis=1)
    abm = jnp.concatenate([jnp.zeros((bsz, lo + 1, 2 * d_f), ab.dtype), mirrored,
                           jnp.zeros((bsz, kp - lo - hl, 2 * d_f), ab.dtype)], axis=1)
    nw = d_f // 2
    col = lambda off: pl.BlockSpec((1, kp, nw), lambda b, j: (b, 0, off + j))
    out = pl.BlockSpec((1, kp, nw), lambda b, j: (b, 0, j))
    ytop, ybot = pl.pallas_call(
        functools.partial(_dft_kernel, scale=scale, n_rows=lo + hl + 1),
        grid=(bsz, 2),
        in_specs=[_resident((kp, kp)), _resident((kp, kp)), col(0), col(2), col(0), col(2)],
        out_specs=[out, out],
        out_shape=[jax.ShapeDtypeStruct((bsz, kp, d_f), BF16)] * 2,
        compiler_params=_params(2),
        name="dft",
    )(ctab, stab, ab, ab, abm, abm)
    return jnp.concatenate([ytop[:, :lo + hl + 1], jnp.flip(ybot[:, lo + 1:lo + hl], axis=1),
                            jnp.zeros((bsz, lp - hi, d_f), BF16)], axis=1)


def _dn_kernel(q_ref, k_ref, v_ref, z_ref, dnc_ref, r_ref, nw_ref, o_ref, acc, st, *, lp, pairs):
    c_sz, d, nh = CHUNK, DN_HEAD_DIM, DN_HEADS
    nc = lp // c_sz
    br = LANE

    ii = lax.broadcasted_iota(jnp.int32, (c_sz, 2 * c_sz), 0)
    jl = lax.broadcasted_iota(jnp.int32, (c_sz, 2 * c_sz), 1)
    left = jl < c_sz
    jj = jnp.where(left, jl, jl - c_sz)
    causal = jnp.where(left, ii - jj, jj - ii)
    tri = causal >= 0
    stri = causal > 0
    eye2 = (ii == jj).astype(F32)
    zeros_c = jnp.zeros((c_sz, 2 * d), BF16)
    zeros_d = jnp.zeros((d, d), BF16)

    def pair(x_f, x_b):
        return jnp.where(left, x_f, x_b)

    def blockdiag(x2):
        return jnp.concatenate([jnp.where(left, x2, 0.0), jnp.where(left, 0.0, x2)], axis=0).astype(BF16)

    def blockrows(x4):
        return jnp.concatenate([jnp.concatenate([x4[:, :2 * d], zeros_c], axis=1),
                                jnp.concatenate([zeros_c, x4[:, 2 * d:]], axis=1)], axis=0)

    acc[...] = jnp.zeros(acc.shape, F32)
    st[...] = jnp.zeros(st.shape, F32)

    def super_iter(it, carry):
        cfs = [it * pairs + p for p in range(pairs)]
        cbs = [nc - 1 - c for c in cfs]
        rf = [pl.multiple_of(c * c_sz, c_sz) for c in cfs]
        rb = [pl.multiple_of(c * c_sz, c_sz) for c in cbs]
        items = [(p, h) for p in range(pairs) for h in range(nh)]
        hs = lambda h: slice(h * d, (h + 1) * d)
        qrow = [(q_ref[0, pl.ds(rf[p], c_sz), :], q_ref[0, pl.ds(rb[p], c_sz), :]) for p in range(pairs)]
        krow = [(k_ref[0, pl.ds(rf[p], c_sz), :], k_ref[0, pl.ds(rb[p], c_sz), :]) for p in range(pairs)]
        vrow = [(v_ref[0, pl.ds(rf[p], c_sz), :], v_ref[0, pl.ds(rb[p], c_sz), :]) for p in range(pairs)]
        drow = [(dnc_ref[0, pl.ds(rf[p], c_sz), :], dnc_ref[0, pl.ds(rb[p], c_sz), :]) for p in range(pairs)]
        qf = [qrow[p][0][:, hs(h)] for p, h in items]
        qb = [qrow[p][1][:, hs(h)] for p, h in items]
        kf = [krow[p][0][:, hs(h)] for p, h in items]
        kb = [krow[p][1][:, hs(h)] for p, h in items]
        vf = [vrow[p][0][:, hs(h)].astype(F32) for p, h in items]
        vb = [vrow[p][1][:, hs(h)].astype(F32) for p, h in items]
        btf = [drow[p][0][:, h:h + 1] for p, h in items]
        btb = [drow[p][1][:, nh + h:nh + h + 1] for p, h in items]
        gcf = [drow[p][0][:, 2 * nh + h:2 * nh + h + 1] for p, h in items]
        gcb = [drow[p][1][:, 3 * nh + h:3 * nh + h + 1] for p, h in items]
        grow = [pair(r_ref[0, h, pl.ds(cfs[p], 1), :], r_ref[0, h, pl.ds(cbs[p], 1), :]) for p, h in items]
        dec = [jnp.where(tri, jnp.exp(jnp.minimum(pair(f, b) - g, 0.0)), 0.0)
               for f, b, g in zip(gcf, gcb, grow)]
        gram = [lax.dot_general(
            jnp.concatenate([jnp.concatenate([a, b], axis=1), jnp.concatenate([c, e], axis=1)], axis=0),
            jnp.concatenate([jnp.concatenate([c, zeros_c[:, :d]], axis=1),
                             jnp.concatenate([zeros_c[:, :d], e], axis=1)], axis=0),
            _NT, preferred_element_type=F32) for a, b, c, e in zip(qf, qb, kf, kb)]
        nm = [jnp.where(stri, -(pair(f, b) * g[c_sz:] * dc), 0.0)
              for f, b, g, dc in zip(btf, btb, gram, dec)]
        sacc = [eye2 + x for x in nm]
        qpow = [_dot(x.astype(BF16), blockdiag(x)) for x in nm]
        for _ in range(4):
            r = [_dot(jnp.concatenate([s, q], axis=0).astype(BF16), blockdiag(q)) for s, q in zip(sacc, qpow)]
            sacc = [s + x[:c_sz] for s, x in zip(sacc, r)]
            qpow = [x[c_sz:] for x in r]
        tinv = [s + _dot(s.astype(BF16), blockdiag(q)) for s, q in zip(sacc, qpow)]
        egf = [jnp.exp(x) for x in gcf]
        egb = [jnp.exp(x) for x in gcb]
        kf32 = [x.astype(F32) for x in kf]
        kb32 = [x.astype(F32) for x in kb]
        rhs = [blockrows(jnp.concatenate([v1 * bf, k1 * (bf * ef), v2 * bb, k2 * (bb * eb)], axis=1).astype(BF16))
               for v1, k1, v2, k2, bf, bb, ef, eb in zip(vf, kf32, vb, kb32, btf, btb, egf, egb)]
        uw = [blockrows(_dot(t.astype(BF16), x).astype(BF16)) for t, x in zip(tinv, rhs)]
        x1 = [_dot((g[:c_sz] * dc).astype(BF16), x) for g, dc, x in zip(gram, dec, uw)]
        gtf = [x[c_sz - 1:c_sz] for x in gcf]
        gtb = [x[0:1] for x in gcb]
        kd = [jnp.concatenate([k1 * jnp.exp(tf - f), k2 * jnp.exp(tb - b)], axis=0).astype(BF16)
              for k1, k2, tf, tb, f, b in zip(kf32, kb32, gtf, gtb, gcf, gcb)]
        x2 = [lax.dot_general(a, x, _TN, preferred_element_type=F32) for a, x in zip(kd, uw)]
        sel = lambda x, o: jnp.concatenate([x[:, o:o + d], x[:, 2 * d + o:3 * d + o]], axis=1)
        lhs = [jnp.concatenate([
            (jnp.concatenate([a.astype(F32) * ef, b.astype(F32) * eb], axis=1) - sel(y1, d)).astype(BF16),
            (-sel(y2, d)).astype(BF16)], axis=0)
            for a, b, ef, eb, y1, y2 in zip(qf, qb, egf, egb, x1, x2)]
        bo = [sel(y1, 0) for y1 in x1]
        bs = [sel(y2, 0) for y2 in x2]
        gl = [jnp.concatenate([jnp.broadcast_to(jnp.exp(tf), (1, d)), jnp.broadcast_to(jnp.exp(tb), (1, d))],
                              axis=1) for tf, tb in zip(gtf, gtb)]

        state = [st[h] for h in range(nh)]
        for n, (p, h) in enumerate(items):
            s2 = state[h]
            sbd = jnp.concatenate([jnp.concatenate([s2[:, :d].astype(BF16), zeros_d], axis=1),
                                   jnp.concatenate([zeros_d, s2[:, d:].astype(BF16)], axis=1)], axis=0)
            r = _dot(lhs[n], sbd)
            o2 = r[:c_sz] + bo[n]
            state[h] = gl[n] * s2 + r[c_sz:] + bs[n]
            acc[pl.ds(rf[p], c_sz), hs(h)] += o2[:, :d]
            acc[pl.ds(rb[p], c_sz), hs(h)] += o2[:, d:]
        for h in range(nh):
            st[h] = state[h]
        return carry

    lax.fori_loop(0, nc // pairs, super_iter, 0)

    def fin(i, carry):
        s = pl.multiple_of(i * br, br)
        zz = z_ref[0, pl.ds(s, br), :].astype(F32)
        for h in range(nh):
            o = acc[pl.ds(s, br), h * d:(h + 1) * d]
            o = o * lax.rsqrt(jnp.mean(o * o, axis=-1, keepdims=True) + EPS) * nw_ref[...]
            o_ref[0, pl.ds(s, br), h * d:(h + 1) * d] = (o * _silu(zz[:, h * d:(h + 1) * d])).astype(BF16)
        return carry

    lax.fori_loop(0, lp // br, fin, 0)


def _deltanet(qkv, z, dnc, rrow, norm_w):
    bsz, lp, _ = qkv.shape
    d, nh = DN_HEAD_DIM, DN_HEADS
    nc = lp // CHUNK
    pairs = next(g for g in (3, 2, 1) if nc % g == 0)
    once = lambda blk, idx: pl.BlockSpec(blk, idx, pipeline_mode=pl.Buffered(1))
    return pl.pallas_call(
        functools.partial(_dn_kernel, lp=lp, pairs=pairs),
        grid=(bsz,),
        in_specs=[once((1, lp, nh * d), lambda b: (b, 0, 0)),
                  once((1, lp, nh * d), lambda b: (b, 0, 1)),
                  once((1, lp, nh * d), lambda b: (b, 0, 2)),
                  once((1, lp, nh * d), lambda b: (b, 0, 0)),
                  pl.BlockSpec((1, lp, LANE), lambda b: (b, 0, 0)),
                  pl.BlockSpec((1, nh, nc, 2 * CHUNK), lambda b: (b, 0, 0, 0)),
                  _resident((1, d))],
        out_specs=pl.BlockSpec((1, lp, nh * d), lambda b: (b, 0, 0)),
        out_shape=jax.ShapeDtypeStruct((bsz, lp, nh * d), BF16),
        scratch_shapes=[pltpu.VMEM((lp, nh * d), F32), pltpu.VMEM((nh, d, 2 * d), F32)],
        compiler_params=_params(1),
        name="deltanet",
    )(qkv, qkv, qkv, z, dnc, rrow, norm_w)


def _merge_kernel(h_ref, y_ref, sc_ref, dn_ref, gate_ref, wf_ref, wsc_ref, wdn_ref, wout_ref, gpost_ref,
                  o_ref, *, d_model):
    y_f = _dot(y_ref[...], wf_ref[...])
    y_sc = _dot(sc_ref[...], wsc_ref[...])
    y_dn = _dot(dn_ref[...], wdn_ref[...])
    merged = (gate_ref[:, 0:d_model].astype(F32) * y_f
              + gate_ref[:, d_model:2 * d_model].astype(F32) * y_sc
              + gate_ref[:, 2 * d_model:3 * d_model].astype(F32) * y_dn)
    out = _dot(merged.astype(BF16), wout_ref[...])
    o_ref[...] = h_ref[...] + _rms(out, gpost_ref[...])


def _merge(h2d, y, sc, dn, gate, wf, wsc, wdn, wout, gpost, tm):
    t, d = h2d.shape
    row = lambda n: pl.BlockSpec((tm, n), lambda i: (i, 0))
    return pl.pallas_call(
        functools.partial(_merge_kernel, d_model=d),
        grid=(t // tm,),
        in_specs=[row(d), row(y.shape[1]), row(sc.shape[1]), row(dn.shape[1]), row(gate.shape[1]),
                  _resident(wf.shape), _resident(wsc.shape), _resident(wdn.shape),
                  _resident(wout.shape), _resident((1, d))],
        out_specs=row(d),
        out_shape=jax.ShapeDtypeStruct((t, d), F32),
        compiler_params=_params(1),
        name="merge",
    )(h2d, y, sc, dn, gate, wf, wsc, wdn, wout, gpost)


def _channel_dft_matrix(d_f):
    gd = FOURIER_GROUP_DIM
    idx = np.arange(gd)
    ang = 2.0 * np.pi * ((idx[:, None] * idx[None, :]) % gd) / gd
    blk = np.eye(d_f // gd)
    return np.concatenate([np.kron(blk, np.cos(ang)), np.kron(blk, -np.sin(ang))], axis=1)


def _pick_tile(n, candidates):
    return next(c for c in candidates if n % c == 0)


def kernel(x, meta_tokens, norm_gains, ffn1_wi, ffn1_wo, ffn2_wi, ffn2_wo, w_in, sconv_w, dn_conv_w,
           dn_A_log, dn_dt_bias, dn_norm, w_fourier, w_sconv_out, w_dn_out, w_out):
    bsz, seq, d = x.shape
    depth = norm_gains.shape[0]
    d_f = w_fourier.shape[1]
    d_sc = w_sconv_out.shape[1]
    d_dn = w_dn_out.shape[1]
    nh = DN_HEADS
    n_log = 4 * nh
    l_real = N_META + seq
    lp = -(-(FRONT_PAD + l_real) // LANE) * LANE
    nc = lp // CHUNK
    t = bsz * lp
    tm = _pick_tile(t, (1024, 768, 512, 384, 256, 128))
    tmi = _pick_tile(t, (1024, 512, 384, 256, 128))
    assert l_real % 2 == 0, "the half-spectrum DFT needs an even number of positions"
    kp = -(-(FRONT_PAD + l_real // 2 + 1) // LANE) * LANE

    meta = jnp.broadcast_to(meta_tokens[None].astype(x.dtype), (bsz, N_META, d))
    h = jnp.concatenate([jnp.zeros((bsz, FRONT_PAD, d), x.dtype), meta, x,
                         jnp.zeros((bsz, lp - FRONT_PAD - l_real, d), x.dtype)], axis=1).reshape(t, d)

    wcs = jnp.asarray(_channel_dft_matrix(d_f), F32).astype(BF16)
    tables = _dft_tables(kp, l_real)
    scale = 1.0 / math.sqrt(l_real * FOURIER_GROUP_DIM)
    o_main = d_f + 3 * d_sc + 4 * d_dn

    for l in range(depth):
        gains = norm_gains[l][:, None, :]
        h = _ffn(h, gains[0], gains[1], ffn1_wi[l].astype(BF16), ffn1_wo[l].astype(BF16), tm)

        w = w_in[l]
        wlog = jnp.pad(w[:, o_main:o_main + n_log], ((0, 0), (0, LANE - n_log))).astype(BF16)
        par = jnp.zeros((8, LANE), F32)
        par = par.at[0, 2 * nh:n_log].set(dn_A_log[l].reshape(-1))
        par = par.at[1, 2 * nh:n_log].set(dn_dt_bias[l].reshape(-1))
        ab, sc, qkv, z, dnc, gate = _inproj(
            h, gains[2], w[:, :o_main].astype(BF16), wlog, w[:, o_main + n_log:].astype(BF16), wcs,
            sconv_w[l], dn_conv_w[l], par, tmi, d_f, d_sc, d_dn, lp, l_real)
        y = _dft(ab.reshape(bsz, lp, 2 * d_f), tables, l_real, d_f, scale)
        r = jnp.transpose(dnc.reshape(bsz, nc, CHUNK, LANE)[..., 2 * nh:n_log], (0, 3, 1, 2))
        rrow = jnp.concatenate([r[:, :nh], r[:, nh:]], axis=-1)
        dn = _deltanet(qkv.reshape(bsz, lp, 3 * d_dn), z.reshape(bsz, lp, d_dn),
                       dnc.reshape(bsz, lp, LANE), rrow, dn_norm[l][None, :])
        h = _merge(h, y.reshape(t, d_f), sc, dn.reshape(t, d_dn), gate,
                   w_fourier[l].astype(BF16), w_sconv_out[l].astype(BF16), w_dn_out[l].astype(BF16),
                   w_out[l].astype(BF16), gains[3], tm)

        h = _ffn(h, gains[4], gains[5], ffn2_wi[l].astype(BF16), ffn2_wo[l].astype(BF16), tm)

    return h.reshape(bsz, lp, d)[:, FRONT_PAD + N_META:FRONT_PAD + l_real]
```

```python
import functools
import math

import numpy as np
import jax
import jax.numpy as jnp
from jax import lax
from jax.experimental import pallas as pl
from jax.experimental.pallas import tpu as pltpu

EPS = 1e-6
N_META = 16
CHUNK = 64
FRONT_PAD = CHUNK - N_META
DN_HEADS = 4
DN_HEAD_DIM = 128
FOURIER_GROUP_DIM = 64
N_BRANCH = 3
LANE = 128
BF16 = jnp.bfloat16
F32 = jnp.float32
VMEM_LIMIT_BYTES = 58 * 1024 * 1024

_NT = (((1,), (1,)), ((), ()))
_TN = (((0,), (0,)), ((), ()))


def _dot(a, b):
    return jnp.dot(a, b, preferred_element_type=F32)


def _rms(x, g):
    ms = jnp.mean(x * x, axis=-1, keepdims=True)
    return x * lax.rsqrt(ms + EPS) * g


def _silu(x):
    return x * jax.nn.sigmoid(x)


def _softplus(x):
    return jnp.maximum(x, 0.0) + jnp.log1p(jnp.exp(-jnp.abs(x)))


def _conv3(x, prev_row, next_row, w):
    n = x.shape[0]
    rows = lax.broadcasted_iota(jnp.int32, (n, 1), 0)
    x_prev = jnp.where(rows == 0, prev_row, pltpu.roll(x, 1, 0))
    x_next = jnp.where(rows == n - 1, next_row, pltpu.roll(x, n - 1, 0))
    return w[0:1] * x_prev + w[1:2] * x + w[2:3] * x_next


def _params(n_axes):
    return pltpu.CompilerParams(dimension_semantics=("arbitrary",) * n_axes,
                                vmem_limit_bytes=VMEM_LIMIT_BYTES)


def _resident(shape):
    zeros = (0,) * len(shape)
    return pl.BlockSpec(shape, lambda *_: zeros, pipeline_mode=pl.Buffered(1))


def _ffn_kernel(h_ref, gpre_ref, gpost_ref, wi_ref, wo_ref, o_ref, hm_ref, *, d_ff, fch):
    x = h_ref[...]
    xn = _rms(x, gpre_ref[...]).astype(BF16)
    for c in range(d_ff // fch):
        a = _dot(xn, wi_ref[:, c * fch:(c + 1) * fch])
        b = _dot(xn, wi_ref[:, d_ff + c * fch:d_ff + (c + 1) * fch])
        hm_ref[:, c * fch:(c + 1) * fch] = (_silu(a) * b).astype(BF16)
    y = _dot(hm_ref[...], wo_ref[...])
    o_ref[...] = x + 0.5 * _rms(y, gpost_ref[...])


def _ffn(h2d, gpre, gpost, wi, wo, tm):
    t, d = h2d.shape
    d_ff = wo.shape[0]
    fch = d_ff // 2 if (d_ff // 2) % LANE == 0 else d_ff
    return pl.pallas_call(
        functools.partial(_ffn_kernel, d_ff=d_ff, fch=fch),
        grid=(t // tm,),
        in_specs=[pl.BlockSpec((tm, d), lambda i: (i, 0)),
                  _resident((1, d)), _resident((1, d)),
                  _resident((d, 2 * d_ff)), _resident((d_ff, d))],
        out_specs=pl.BlockSpec((tm, d), lambda i: (i, 0)),
        out_shape=jax.ShapeDtypeStruct((t, d), F32),
        scratch_shapes=[pltpu.VMEM((tm, d_ff), BF16)],
        compiler_params=_params(1),
        name="ffn",
    )(h2d, gpre, gpost, wi, wo)


def _inproj_kernel(h_ref, hp_ref, hn_ref, g_ref, wmain_ref, wlog_ref, wgate_ref, wcs_ref, scw_ref, cw_ref,
                   par_ref, ab_ref, sc_ref, qkv_ref, z_ref, dnc_ref, gate_ref,
                   *, d_f, d_sc, d_dn, lp, l_real):
    i = pl.program_id(0)
    tm = h_ref.shape[0]
    d, nh = DN_HEAD_DIM, DN_HEADS
    gain = g_ref[...]
    xn = _rms(h_ref[...], gain).astype(BF16)
    o_sc, o_qkv, o_z = d_f, d_f + 3 * d_sc, d_f + 3 * d_sc + 3 * d_dn

    xh = _rms(jnp.concatenate([hp_ref[...], hn_ref[...]], axis=0), gain).astype(BF16)
    ph = _dot(xh, wmain_ref[:, o_sc:o_z])
    prev = jnp.where(i > 0, ph[7:8], 0.0)
    nxt = jnp.where(i < pl.num_programs(0) - 1, ph[8:9], 0.0)

    uf = _dot(xn, wmain_ref[:, 0:d_f])
    ab_ref[...] = _dot(uf.astype(BF16), wcs_ref[...]).astype(BF16)

    usc = _dot(xn, wmain_ref[:, o_sc:o_qkv])
    ch = usc[:, d_sc:2 * d_sc] * usc[:, 2 * d_sc:]
    ch_prev = prev[:, d_sc:2 * d_sc] * prev[:, 2 * d_sc:3 * d_sc]
    ch_next = nxt[:, d_sc:2 * d_sc] * nxt[:, 2 * d_sc:3 * d_sc]
    sc_ref[...] = (usc[:, :d_sc] * _conv3(ch, ch_prev, ch_next, scw_ref[...])).astype(BF16)

    qkv = _dot(xn, wmain_ref[:, o_qkv:o_z])
    y = _silu(_conv3(qkv, prev[:, 3 * d_sc:], nxt[:, 3 * d_sc:], cw_ref[...]))
    for blk in range(3 * nh):
        yb = y[:, blk * d:(blk + 1) * d]
        if blk < 2 * nh:
            yb = yb * lax.rsqrt(jnp.sum(yb * yb, axis=-1, keepdims=True) + EPS)
        if blk < nh:
            yb = yb * (d ** -0.5)
        qkv_ref[:, blk * d:(blk + 1) * d] = yb.astype(BF16)

    z_ref[...] = _dot(xn, wmain_ref[:, o_z:o_z + d_dn]).astype(BF16)
    gate_ref[...] = jax.nn.sigmoid(_dot(xn, wgate_ref[...])).astype(BF16)

    lg = _dot(xn, wlog_ref[...])
    rows = lax.broadcasted_iota(jnp.int32, (tm, 1), 0)
    lane = lax.broadcasted_iota(jnp.int32, (1, LANE), 1)
    pos = i * tm + rows - ((i * tm) // lp) * lp
    pos = jnp.where(pos >= lp, pos - lp, pos)
    beta = jax.nn.sigmoid(lg)
    g = -jnp.exp(par_ref[0:1, :]) * _softplus(lg + par_ref[1:2, :])
    bg = jnp.where(lane < 2 * nh, beta, g)
    bg = jnp.where(pos >= FRONT_PAD, bg, 0.0)
    bg = jnp.where(pos < FRONT_PAD + l_real, bg, 0.0)
    rc = rows & (CHUNK - 1)
    pre = bg
    suf = bg
    s = 1
    while s < CHUNK:
        pre = pre + jnp.where(rc >= s, pltpu.roll(pre, s, 0), 0.0)
        suf = suf + jnp.where(rc < CHUNK - s, pltpu.roll(suf, tm - s, 0), 0.0)
        s *= 2
    dnc_ref[...] = jnp.where(lane < 2 * nh, bg, jnp.where(lane < 3 * nh, pre, suf))


def _inproj(h2d, g, wmain, wlog, wgate, wcs, scw, cw, par, tm, d_f, d_sc, d_dn, lp, l_real):
    t, d = h2d.shape
    row = lambda n: pl.BlockSpec((tm, n), lambda i: (i, 0))
    hb = tm // 8
    last = t // 8 - 1
    outs = [(2 * d_f, BF16), (d_sc, BF16), (3 * d_dn, BF16), (d_dn, BF16), (LANE, F32),
            (wgate.shape[1], BF16)]
    return pl.pallas_call(
        functools.partial(_inproj_kernel, d_f=d_f, d_sc=d_sc, d_dn=d_dn, lp=lp, l_real=l_real),
        grid=(t // tm,),
        in_specs=[row(d),
                  pl.BlockSpec((8, d), lambda i: (jnp.maximum(i * hb - 1, 0), 0)),
                  pl.BlockSpec((8, d), lambda i: (jnp.minimum((i + 1) * hb, last), 0)),
                  _resident((1, d)), _resident(wmain.shape), _resident(wlog.shape),
                  _resident(wgate.shape), _resident(wcs.shape), _resident(scw.shape), _resident(cw.shape),
                  _resident(par.shape)],
        out_specs=[row(n) for n, _ in outs],
        out_shape=[jax.ShapeDtypeStruct((t, n), dt) for n, dt in outs],
        compiler_params=_params(1),
        name="inproj",
    )(h2d, h2d, h2d, g, wmain, wlog, wgate, wcs, scw, cw, par)


def _dft_table_kernel(pc_ref, ps_ref, qc_ref, qs_ref, c_ref, s_ref):
    pc = pc_ref[...]
    ps = ps_ref[...]
    for nb in range(c_ref.shape[1] // LANE):
        qc = qc_ref[:, nb:nb + 1]
        qs = qs_ref[:, nb:nb + 1]
        c_ref[:, nb * LANE:(nb + 1) * LANE] = (qc * pc - qs * ps).astype(BF16)
        s_ref[:, nb * LANE:(nb + 1) * LANE] = (qs * pc + qc * ps).astype(BF16)


def _dft_tables(kp, l_real):
    k = jnp.arange(kp, dtype=jnp.int32) - FRONT_PAD
    valid = (k >= 0) & (k <= l_real // 2)
    kk = jnp.where(valid, k, 0)[:, None]
    theta = 2.0 * math.pi / l_real
    j = jnp.arange(LANE, dtype=jnp.int32)[None, :]
    rp = ((kk * j) % l_real).astype(F32) * theta
    ob = (jnp.arange(LANE, dtype=jnp.int32) * LANE - FRONT_PAD) % l_real
    rq = ((kk * ob[None, :]) % l_real).astype(F32) * theta
    qmask = valid[:, None] & (jnp.arange(LANE)[None, :] < kp // LANE)
    seeds = (jnp.cos(rp), jnp.sin(rp), jnp.where(qmask, jnp.cos(rq), 0.0), jnp.where(qmask, jnp.sin(rq), 0.0))
    tr = LANE
    seed = pl.BlockSpec((tr, LANE), lambda m: (m, 0))
    tab = pl.BlockSpec((tr, kp), lambda m: (m, 0))
    return pl.pallas_call(
        _dft_table_kernel,
        grid=(kp // tr,),
        in_specs=[seed] * 4,
        out_specs=[tab, tab],
        out_shape=[jax.ShapeDtypeStruct((kp, kp), BF16)] * 2,
        compiler_params=_params(1),
        name="dft_tables",
    )(*seeds)


def _dft_kernel(c_ref, s_ref, a_ref, b_ref, y_ref, ae_scr, bo_scr, yb_scr, *, scale, l_real):
    lp = a_ref.shape[1]
    kp = c_ref.shape[0]
    nb, nbk = lp // LANE, kp // LANE
    lo, hl = FRONT_PAD, l_real // 2
    q0, rem = divmod(2 * lo + l_real - (LANE - 1), LANE)
    jr = lax.broadcasted_iota(jnp.int32, (LANE, 2 * LANE), 0)
    jc = lax.broadcasted_iota(jnp.int32, (LANE, 2 * LANE), 1)
    mirror = (jr + jc == rem + LANE - 1).astype(BF16)
    rows = lax.broadcasted_iota(jnp.int32, (LANE, 1), 0)

    def mirrored(ref, n_blocks, m):
        blocks = [q0 - m, q0 - m + 1]
        if all(bi < 0 or bi >= n_blocks for bi in blocks):
            return None

        def block(bi):
            if not 0 <= bi < n_blocks:
                return jnp.zeros((LANE, ref.shape[-1]), BF16)
            sl = slice(bi * LANE, (bi + 1) * LANE)
            return ref[0, sl, :] if len(ref.shape) == 3 else ref[sl, :]

        return _dot(mirror, jnp.concatenate([block(bi) for bi in blocks], axis=0))

    def interior(rg, x):
        return jnp.where(rg > lo, jnp.where(rg < lo + hl, x, 0.0), 0.0)

    for m in range(nbk):
        rg = m * LANE + rows
        sl = slice(m * LANE, (m + 1) * LANE)
        for src, dst, sign in ((a_ref, ae_scr, 1.0), (b_ref, bo_scr, -1.0)):
            x = src[0, sl, :].astype(F32)
            xm = mirrored(src, nb, m)
            if xm is not None:
                x = x + sign * interior(rg, xm)
            dst[sl, :] = jnp.where(rg <= lo + hl, x, 0.0).astype(BF16)

    p = _dot(c_ref[...], ae_scr[...])
    q = _dot(s_ref[...], bo_scr[...])
    yb_scr[...] = interior(lax.broadcasted_iota(jnp.int32, (kp, 1), 0), (p - q) * scale).astype(BF16)
    ytop = (p + q) * scale
    for m in range(nb):
        y = ytop[m * LANE:(m + 1) * LANE] if m < nbk else None
        ym = mirrored(yb_scr, nbk, m)
        if y is None and ym is None:
            y = jnp.zeros((LANE, y_ref.shape[-1]), F32)
        elif y is None:
            y = ym
        elif ym is not None:
            y = y + ym
        y_ref[0, m * LANE:(m + 1) * LANE, :] = y.astype(BF16)


def _dft(ab, tables, l_real, d_f, scale):
    bsz, lp, _ = ab.shape
    ctab, stab = tables
    kp = ctab.shape[0]
    nw = d_f // 2
    col = lambda off: pl.BlockSpec((1, lp, nw), lambda b, j: (b, 0, off + j))
    return pl.pallas_call(
        functools.partial(_dft_kernel, scale=scale, l_real=l_real),
        grid=(bsz, 2),
        in_specs=[_resident((kp, kp)), _resident((kp, kp)), col(0), col(2)],
        out_specs=col(0),
        out_shape=jax.ShapeDtypeStruct((bsz, lp, d_f), BF16),
        scratch_shapes=[pltpu.VMEM((kp, nw), BF16)] * 3,
        compiler_params=_params(2),
        name="dft",
    )(ctab, stab, ab, ab)


def _dn_kernel(q_ref, k_ref, v_ref, z_ref, dnc_ref, r_ref, nw_ref, o_ref, acc, st, *, lp, pairs):
    c_sz, d, nh = CHUNK, DN_HEAD_DIM, DN_HEADS
    nc = lp // c_sz
    br = LANE

    ii = lax.broadcasted_iota(jnp.int32, (c_sz, 2 * c_sz), 0)
    jl = lax.broadcasted_iota(jnp.int32, (c_sz, 2 * c_sz), 1)
    left = jl < c_sz
    jj = jnp.where(left, jl, jl - c_sz)
    causal = jnp.where(left, ii - jj, jj - ii)
    tri = causal >= 0
    stri = causal > 0
    eye2 = (ii == jj).astype(F32)
    zeros_c = jnp.zeros((c_sz, 2 * d), BF16)
    zeros_d = jnp.zeros((d, d), BF16)

    def pair(x_f, x_b):
        return jnp.where(left, x_f, x_b)

    def blockdiag(x2):
        return jnp.concatenate([jnp.where(left, x2, 0.0), jnp.where(left, 0.0, x2)], axis=0).astype(BF16)

    def blockrows(x4):
        return jnp.concatenate([jnp.concatenate([x4[:, :2 * d], zeros_c], axis=1),
                                jnp.concatenate([zeros_c, x4[:, 2 * d:]], axis=1)], axis=0)

    acc[...] = jnp.zeros(acc.shape, F32)
    st[...] = jnp.zeros(st.shape, F32)

    def super_iter(it, carry):
        cfs = [it * pairs + p for p in range(pairs)]
        cbs = [nc - 1 - c for c in cfs]
        rf = [pl.multiple_of(c * c_sz, c_sz) for c in cfs]
        rb = [pl.multiple_of(c * c_sz, c_sz) for c in cbs]
        items = [(p, h) for p in range(pairs) for h in range(nh)]
        hs = lambda h: slice(h * d, (h + 1) * d)
        qrow = [(q_ref[0, pl.ds(rf[p], c_sz), :], q_ref[0, pl.ds(rb[p], c_sz), :]) for p in range(pairs)]
        krow = [(k_ref[0, pl.ds(rf[p], c_sz), :], k_ref[0, pl.ds(rb[p], c_sz), :]) for p in range(pairs)]
        vrow = [(v_ref[0, pl.ds(rf[p], c_sz), :], v_ref[0, pl.ds(rb[p], c_sz), :]) for p in range(pairs)]
        drow = [(dnc_ref[0, pl.ds(rf[p], c_sz), :], dnc_ref[0, pl.ds(rb[p], c_sz), :]) for p in range(pairs)]
        qf = [qrow[p][0][:, hs(h)] for p, h in items]
        qb = [qrow[p][1][:, hs(h)] for p, h in items]
        kf = [krow[p][0][:, hs(h)] for p, h in items]
        kb = [krow[p][1][:, hs(h)] for p, h in items]
        vf = [vrow[p][0][:, hs(h)].astype(F32) for p, h in items]
        vb = [vrow[p][1][:, hs(h)].astype(F32) for p, h in items]
        btf = [drow[p][0][:, h:h + 1] for p, h in items]
        btb = [drow[p][1][:, nh + h:nh + h + 1] for p, h in items]
        gcf = [drow[p][0][:, 2 * nh + h:2 * nh + h + 1] for p, h in items]
        gcb = [drow[p][1][:, 3 * nh + h:3 * nh + h + 1] for p, h in items]
        grow = [pair(r_ref[0, h, pl.ds(cfs[p], 1), :], r_ref[0, h, pl.ds(cbs[p], 1), :]) for p, h in items]
        dec = [jnp.where(tri, jnp.exp(jnp.minimum(pair(f, b) - g, 0.0)), 0.0)
               for f, b, g in zip(gcf, gcb, grow)]
        gram = [lax.dot_general(
            jnp.concatenate([jnp.concatenate([a, b], axis=1), jnp.concatenate([c, e], axis=1)], axis=0),
            jnp.concatenate([jnp.concatenate([c, zeros_c[:, :d]], axis=1),
                             jnp.concatenate([zeros_c[:, :d], e], axis=1)], axis=0),
            _NT, preferred_element_type=F32) for a, b, c, e in zip(qf, qb, kf, kb)]
        nm = [jnp.where(stri, -(pair(f, b) * g[c_sz:] * dc), 0.0)
              for f, b, g, dc in zip(btf, btb, gram, dec)]
        sacc = [eye2 + x for x in nm]
        qpow = [_dot(x.astype(BF16), blockdiag(x)) for x in nm]
        for _ in range(4):
            r = [_dot(jnp.concatenate([s, q], axis=0).astype(BF16), blockdiag(q)) for s, q in zip(sacc, qpow)]
            sacc = [s + x[:c_sz] for s, x in zip(sacc, r)]
            qpow = [x[c_sz:] for x in r]
        tinv = [s + _dot(s.astype(BF16), blockdiag(q)) for s, q in zip(sacc, qpow)]
        egf = [jnp.exp(x) for x in gcf]
        egb = [jnp.exp(x) for x in gcb]
        kf32 = [x.astype(F32) for x in kf]
        kb32 = [x.astype(F32) for x in kb]
        rhs = [blockrows(jnp.concatenate([v1 * bf, k1 * (bf * ef), v2 * bb, k2 * (bb * eb)], axis=1).astype(BF16))
               for v1, k1, v2, k2, bf, bb, ef, eb in zip(vf, kf32, vb, kb32, btf, btb, egf, egb)]
        uw = [blockrows(_dot(t.astype(BF16), x).astype(BF16)) for t, x in zip(tinv, rhs)]
        x1 = [_dot((g[:c_sz] * dc).astype(BF16), x) for g, dc, x in zip(gram, dec, uw)]
        gtf = [x[c_sz - 1:c_sz] for x in gcf]
        gtb = [x[0:1] for x in gcb]
        kd = [jnp.concatenate([k1 * jnp.exp(tf - f), k2 * jnp.exp(tb - b)], axis=0).astype(BF16)
              for k1, k2, tf, tb, f, b in zip(kf32, kb32, gtf, gtb, gcf, gcb)]
        x2 = [lax.dot_general(a, x, _TN, preferred_element_type=F32) for a, x in zip(kd, uw)]
        sel = lambda x, o: jnp.concatenate([x[:, o:o + d], x[:, 2 * d + o:3 * d + o]], axis=1)
        lhs = [jnp.concatenate([
            (jnp.concatenate([a.astype(F32) * ef, b.astype(F32) * eb], axis=1) - sel(y1, d)).astype(BF16),
            (-sel(y2, d)).astype(BF16)], axis=0)
            for a, b, ef, eb, y1, y2 in zip(qf, qb, egf, egb, x1, x2)]
        bo = [sel(y1, 0) for y1 in x1]
        bs = [sel(y2, 0) for y2 in x2]
        gl = [jnp.concatenate([jnp.broadcast_to(jnp.exp(tf), (1, d)), jnp.broadcast_to(jnp.exp(tb), (1, d))],
                              axis=1) for tf, tb in zip(gtf, gtb)]

        state = [st[h] for h in range(nh)]
        for n, (p, h) in enumerate(items):
            s2 = state[h]
            sbd = jnp.concatenate([jnp.concatenate([s2[:, :d].astype(BF16), zeros_d], axis=1),
                                   jnp.concatenate([zeros_d, s2[:, d:].astype(BF16)], axis=1)], axis=0)
            r = _dot(lhs[n], sbd)
            o2 = r[:c_sz] + bo[n]
            state[h] = gl[n] * s2 + r[c_sz:] + bs[n]
            acc[pl.ds(rf[p], c_sz), hs(h)] += o2[:, :d]
            acc[pl.ds(rb[p], c_sz), hs(h)] += o2[:, d:]
        for h in range(nh):
            st[h] = state[h]
        return carry

    lax.fori_loop(0, nc // pairs, super_iter, 0)

    def fin(i, carry):
        s = pl.multiple_of(i * br, br)
        zz = z_ref[0, pl.ds(s, br), :].astype(F32)
        for h in range(nh):
            o = acc[pl.ds(s, br), h * d:(h + 1) * d]
            o = o * lax.rsqrt(jnp.mean(o * o, axis=-1, keepdims=True) + EPS) * nw_ref[...]
            o_ref[0, pl.ds(s, br), h * d:(h + 1) * d] = (o * _silu(zz[:, h * d:(h + 1) * d])).astype(BF16)
        return carry

    lax.fori_loop(0, lp // br, fin, 0)


def _deltanet(qkv, z, dnc, rrow, norm_w):
    bsz, lp, _ = qkv.shape
    d, nh = DN_HEAD_DIM, DN_HEADS
    nc = lp // CHUNK
    pairs = next(g for g in (3, 2, 1) if nc % g == 0)
    once = lambda blk, idx: pl.BlockSpec(blk, idx, pipeline_mode=pl.Buffered(1))
    return pl.pallas_call(
        functools.partial(_dn_kernel, lp=lp, pairs=pairs),
        grid=(bsz,),
        in_specs=[once((1, lp, nh * d), lambda b: (b, 0, 0)),
                  once((1, lp, nh * d), lambda b: (b, 0, 1)),
                  once((1, lp, nh * d), lambda b: (b, 0, 2)),
                  once((1, lp, nh * d), lambda b: (b, 0, 0)),
                  pl.BlockSpec((1, lp, LANE), lambda b: (b, 0, 0)),
                  pl.BlockSpec((1, nh, nc, 2 * CHUNK), lambda b: (b, 0, 0, 0)),
                  _resident((1, d))],
        out_specs=pl.BlockSpec((1, lp, nh * d), lambda b: (b, 0, 0)),
        out_shape=jax.ShapeDtypeStruct((bsz, lp, nh * d), BF16),
        scratch_shapes=[pltpu.VMEM((lp, nh * d), F32), pltpu.VMEM((nh, d, 2 * d), F32)],
        compiler_params=_params(1),
        name="deltanet",
    )(qkv, qkv, qkv, z, dnc, rrow, norm_w)


def _merge_kernel(h_ref, y_ref, sc_ref, dn_ref, gate_ref, wf_ref, wsc_ref, wdn_ref, wout_ref, gpost_ref,
                  o_ref, *, d_model):
    y_f = _dot(y_ref[...], wf_ref[...])
    y_sc = _dot(sc_ref[...], wsc_ref[...])
    y_dn = _dot(dn_ref[...], wdn_ref[...])
    merged = (gate_ref[:, 0:d_model].astype(F32) * y_f
              + gate_ref[:, d_model:2 * d_model].astype(F32) * y_sc
              + gate_ref[:, 2 * d_model:3 * d_model].astype(F32) * y_dn)
    out = _dot(merged.astype(BF16), wout_ref[...])
    o_ref[...] = h_ref[...] + _rms(out, gpost_ref[...])


def _merge(h2d, y, sc, dn, gate, wf, wsc, wdn, wout, gpost, tm):
    t, d = h2d.shape
    row = lambda n: pl.BlockSpec((tm, n), lambda i: (i, 0))
    return pl.pallas_call(
        functools.partial(_merge_kernel, d_model=d),
        grid=(t // tm,),
        in_specs=[row(d), row(y.shape[1]), row(sc.shape[1]), row(dn.shape[1]), row(gate.shape[1]),
                  _resident(wf.shape), _resident(wsc.shape), _resident(wdn.shape),
                  _resident(wout.shape), _resident((1, d))],
        out_specs=row(d),
        out_shape=jax.ShapeDtypeStruct((t, d), F32),
        compiler_params=_params(1),
        name="merge",
    )(h2d, y, sc, dn, gate, wf, wsc, wdn, wout, gpost)


def _channel_dft_matrix(d_f):
    gd = FOURIER_GROUP_DIM
    idx = np.arange(gd)
    ang = 2.0 * np.pi * ((idx[:, None] * idx[None, :]) % gd) / gd
    blk = np.eye(d_f // gd)
    return np.concatenate([np.kron(blk, np.cos(ang)), np.kron(blk, -np.sin(ang))], axis=1)


def _pick_tile(n, candidates):
    return next(c for c in candidates if n % c == 0)


def kernel(x, meta_tokens, norm_gains, ffn1_wi, ffn1_wo, ffn2_wi, ffn2_wo, w_in, sconv_w, dn_conv_w,
           dn_A_log, dn_dt_bias, dn_norm, w_fourier, w_sconv_out, w_dn_out, w_out):
    bsz, seq, d = x.shape
    depth = norm_gains.shape[0]
    d_f = w_fourier.shape[1]
    d_sc = w_sconv_out.shape[1]
    d_dn = w_dn_out.shape[1]
    nh = DN_HEADS
    n_log = 4 * nh
    l_real = N_META + seq
    lp = -(-(FRONT_PAD + l_real) // LANE) * LANE
    nc = lp // CHUNK
    t = bsz * lp
    tm = _pick_tile(t, (1024, 768, 512, 384, 256, 128))
    tmi = _pick_tile(t, (1024, 512, 384, 256, 128))
    assert l_real % 2 == 0, "the half-spectrum DFT needs an even number of positions"
    kp = -(-(FRONT_PAD + l_real // 2 + 1) // LANE) * LANE

    meta = jnp.broadcast_to(meta_tokens[None].astype(x.dtype), (bsz, N_META, d))
    h = jnp.concatenate([jnp.zeros((bsz, FRONT_PAD, d), x.dtype), meta, x,
                         jnp.zeros((bsz, lp - FRONT_PAD - l_real, d), x.dtype)], axis=1).reshape(t, d)

    wcs = jnp.asarray(_channel_dft_matrix(d_f), F32).astype(BF16)
    tables = _dft_tables(kp, l_real)
    scale = 1.0 / math.sqrt(l_real * FOURIER_GROUP_DIM)
    o_main = d_f + 3 * d_sc + 4 * d_dn

    for l in range(depth):
        gains = norm_gains[l][:, None, :]
        h = _ffn(h, gains[0], gains[1], ffn1_wi[l].astype(BF16), ffn1_wo[l].astype(BF16), tm)

        w = w_in[l]
        wlog = jnp.pad(w[:, o_main:o_main + n_log], ((0, 0), (0, LANE - n_log))).astype(BF16)
        par = jnp.zeros((8, LANE), F32)
        par = par.at[0, 2 * nh:n_log].set(dn_A_log[l].reshape(-1))
        par = par.at[1, 2 * nh:n_log].set(dn_dt_bias[l].reshape(-1))
        ab, sc, qkv, z, dnc, gate = _inproj(
            h, gains[2], w[:, :o_main].astype(BF16), wlog, w[:, o_main + n_log:].astype(BF16), wcs,
            sconv_w[l], dn_conv_w[l], par, tmi, d_f, d_sc, d_dn, lp, l_real)
        y = _dft(ab.reshape(bsz, lp, 2 * d_f), tables, l_real, d_f, scale)
        r = jnp.transpose(dnc.reshape(bsz, nc, CHUNK, LANE)[..., 2 * nh:n_log], (0, 3, 1, 2))
        rrow = jnp.concatenate([r[:, :nh], r[:, nh:]], axis=-1)
        dn = _deltanet(qkv.reshape(bsz, lp, 3 * d_dn), z.reshape(bsz, lp, d_dn),
                       dnc.reshape(bsz, lp, LANE), rrow, dn_norm[l][None, :])
        h = _merge(h, y.reshape(t, d_f), sc, dn.reshape(t, d_dn), gate,
                   w_fourier[l].astype(BF16), w_sconv_out[l].astype(BF16), w_dn_out[l].astype(BF16),
                   w_out[l].astype(BF16), gains[3], tm)

        h = _ffn(h, gains[4], gains[5], ffn2_wi[l].astype(BF16), ffn2_wo[l].astype(BF16), tm)

    return h.reshape(bsz, lp, d)[:, FRONT_PAD + N_META:FRONT_PAD + l_real]
```

```python
import functools
import math

import numpy as np
import jax
import jax.numpy as jnp
from jax import lax
from jax.experimental import pallas as pl
from jax.experimental.pallas import tpu as pltpu

EPS = 1e-6
N_META = 16
CHUNK = 64
FRONT_PAD = CHUNK - N_META
DN_HEADS = 4
DN_HEAD_DIM = 128
FOURIER_GROUP_DIM = 64
N_BRANCH = 3
LANE = 128
BF16 = jnp.bfloat16
F32 = jnp.float32
VMEM_LIMIT_BYTES = 58 * 1024 * 1024

_NT = (((1,), (1,)), ((), ()))
_TN = (((0,), (0,)), ((), ()))


def _dot(a, b):
    return jnp.dot(a, b, preferred_element_type=F32)


def _rms(x, g):
    ms = jnp.mean(x * x, axis=-1, keepdims=True)
    return x * lax.rsqrt(ms + EPS) * g


def _silu(x):
    return x * jax.nn.sigmoid(x)


def _softplus(x):
    return jnp.maximum(x, 0.0) + jnp.log1p(jnp.exp(-jnp.abs(x)))


def _conv3(x, prev_row, next_row, w):
    n = x.shape[0]
    rows = lax.broadcasted_iota(jnp.int32, (n, 1), 0)
    x_prev = jnp.where(rows == 0, prev_row, pltpu.roll(x, 1, 0))
    x_next = jnp.where(rows == n - 1, next_row, pltpu.roll(x, n - 1, 0))
    return w[0:1] * x_prev + w[1:2] * x + w[2:3] * x_next


def _params(n_axes):
    return pltpu.CompilerParams(dimension_semantics=("arbitrary",) * n_axes,
                                vmem_limit_bytes=VMEM_LIMIT_BYTES)


def _resident(shape):
    zeros = (0,) * len(shape)
    return pl.BlockSpec(shape, lambda *_: zeros, pipeline_mode=pl.Buffered(1))


def _ffn_kernel(h_ref, gpre_ref, gpost_ref, wi_ref, wo_ref, o_ref, hm_ref, *, d_ff, fch):
    x = h_ref[...]
    xn = _rms(x, gpre_ref[...]).astype(BF16)
    for c in range(d_ff // fch):
        a = _dot(xn, wi_ref[:, c * fch:(c + 1) * fch])
        b = _dot(xn, wi_ref[:, d_ff + c * fch:d_ff + (c + 1) * fch])
        hm_ref[:, c * fch:(c + 1) * fch] = (_silu(a) * b).astype(BF16)
    y = _dot(hm_ref[...], wo_ref[...])
    o_ref[...] = x + 0.5 * _rms(y, gpost_ref[...])


def _ffn(h2d, gpre, gpost, wi, wo, tm, unpad=None, pad=None):
    t, d = h2d.shape
    d_ff = wo.shape[0]
    fch = next(c for c in (512, 256, d_ff) if d_ff % c == 0)
    if unpad is None and pad is None:
        grid = (t // tm,)
        x_spec = pl.BlockSpec((tm, d), lambda i: (i, 0))
        o_spec = pl.BlockSpec((tm, d), lambda i: (i, 0))
        rows_out = t
    else:
        bsz, lp, first, count = unpad or pad
        grid = (bsz, count // tm)
        assert lp % 8 == 0 and first % 8 == 0 and tm % 8 == 0
        padded = pl.BlockSpec((pl.Element(tm), pl.Element(d)),
                              lambda b, j: (pl.multiple_of(b * lp + first + j * tm, 8), 0))
        dense = pl.BlockSpec((tm, d), lambda b, j: (b * (count // tm) + j, 0))
        x_spec, o_spec = (padded, dense) if unpad else (dense, padded)
        rows_out = bsz * (count if unpad else lp)
    weights = [_resident((1, d)), _resident((1, d)), _resident((d, 2 * d_ff)), _resident((d_ff, d))]
    return pl.pallas_call(
        functools.partial(_ffn_kernel, d_ff=d_ff, fch=fch),
        grid=grid,
        in_specs=[x_spec] + weights,
        out_specs=o_spec,
        out_shape=jax.ShapeDtypeStruct((rows_out, d), F32),
        scratch_shapes=[pltpu.VMEM((tm, d_ff), BF16)],
        compiler_params=_params(len(grid)),
        name="ffn",
    )(h2d, gpre, gpost, wi, wo)


def _edge_kernel(h_ref, head_ref, o_ref):
    del h_ref
    o_ref[...] = jnp.where(pl.program_id(1) == 0, head_ref[...], 0.0)


def _fill_edges(h2d, head, bsz, lp, tail):
    t, d = h2d.shape
    assert tail in (0, CHUNK)
    nblk = lp // CHUNK
    return pl.pallas_call(
        _edge_kernel,
        grid=(bsz, 1 + (tail > 0)),
        in_specs=[pl.BlockSpec(memory_space=pl.ANY), _resident((CHUNK, d))],
        out_specs=pl.BlockSpec((CHUNK, d), lambda b, j: (b * nblk + j * (nblk - 1), 0)),
        out_shape=jax.ShapeDtypeStruct((t, d), F32),
        input_output_aliases={0: 0},
        compiler_params=_params(2),
        name="fill_edges",
    )(h2d, head)


def _inproj_kernel(h_ref, hp_ref, hn_ref, g_ref, wmain_ref, wlog_ref, wgate_ref, wcs_ref, scw_ref, cw_ref,
                   par_ref, ab_ref, sc_ref, qkv_ref, z_ref, dnc_ref, gate_ref,
                   *, d_f, d_sc, d_dn, lp, l_real):
    i = pl.program_id(0)
    tm = h_ref.shape[0]
    d, nh = DN_HEAD_DIM, DN_HEADS
    gain = g_ref[...]
    xn = _rms(h_ref[...], gain).astype(BF16)
    o_sc, o_qkv, o_z = d_f, d_f + 3 * d_sc, d_f + 3 * d_sc + 3 * d_dn

    xh = _rms(jnp.concatenate([hp_ref[...], hn_ref[...]], axis=0), gain).astype(BF16)
    ph = _dot(xh, wmain_ref[:, o_sc:o_z])
    prev = jnp.where(i > 0, ph[7:8], 0.0)
    nxt = jnp.where(i < pl.num_programs(0) - 1, ph[8:9], 0.0)

    uf = _dot(xn, wmain_ref[:, 0:d_f])
    ab_ref[...] = _dot(uf.astype(BF16), wcs_ref[...]).astype(BF16)

    usc = _dot(xn, wmain_ref[:, o_sc:o_qkv])
    ch = usc[:, d_sc:2 * d_sc] * usc[:, 2 * d_sc:]
    ch_prev = prev[:, d_sc:2 * d_sc] * prev[:, 2 * d_sc:3 * d_sc]
    ch_next = nxt[:, d_sc:2 * d_sc] * nxt[:, 2 * d_sc:3 * d_sc]
    sc_ref[...] = (usc[:, :d_sc] * _conv3(ch, ch_prev, ch_next, scw_ref[...])).astype(BF16)

    qkv = _dot(xn, wmain_ref[:, o_qkv:o_z])
    y = _silu(_conv3(qkv, prev[:, 3 * d_sc:], nxt[:, 3 * d_sc:], cw_ref[...]))
    for blk in range(3 * nh):
        yb = y[:, blk * d:(blk + 1) * d]
        if blk < 2 * nh:
            yb = yb * lax.rsqrt(jnp.sum(yb * yb, axis=-1, keepdims=True) + EPS)
        if blk < nh:
            yb = yb * (d ** -0.5)
        qkv_ref[:, blk * d:(blk + 1) * d] = yb.astype(BF16)

    z_ref[...] = _dot(xn, wmain_ref[:, o_z:o_z + d_dn]).astype(BF16)
    gate_ref[...] = jax.nn.sigmoid(_dot(xn, wgate_ref[...])).astype(BF16)

    lg = _dot(xn, wlog_ref[...])
    rows = lax.broadcasted_iota(jnp.int32, (tm, 1), 0)
    lane = lax.broadcasted_iota(jnp.int32, (1, LANE), 1)
    pos = i * tm + rows - ((i * tm) // lp) * lp
    pos = jnp.where(pos >= lp, pos - lp, pos)
    beta = jax.nn.sigmoid(lg)
    g = -jnp.exp(par_ref[0:1, :]) * _softplus(lg + par_ref[1:2, :])
    bg = jnp.where(lane < 2 * nh, beta, g)
    bg = jnp.where(pos >= FRONT_PAD, bg, 0.0)
    bg = jnp.where(pos < FRONT_PAD + l_real, bg, 0.0)
    rc = rows & (CHUNK - 1)
    pre = bg
    suf = bg
    s = 1
    while s < CHUNK:
        pre = pre + jnp.where(rc >= s, pltpu.roll(pre, s, 0), 0.0)
        suf = suf + jnp.where(rc < CHUNK - s, pltpu.roll(suf, tm - s, 0), 0.0)
        s *= 2
    dnc_ref[...] = jnp.where(lane < 2 * nh, bg, jnp.where(lane < 3 * nh, pre, suf))


def _inproj(h2d, g, wmain, wlog, wgate, wcs, scw, cw, par, tm, d_f, d_sc, d_dn, lp, l_real):
    t, d = h2d.shape
    row = lambda n: pl.BlockSpec((tm, n), lambda i: (i, 0))
    hb = tm // 8
    last = t // 8 - 1
    outs = [(2 * d_f, BF16), (d_sc, BF16), (3 * d_dn, BF16), (d_dn, BF16), (LANE, F32),
            (wgate.shape[1], BF16)]
    return pl.pallas_call(
        functools.partial(_inproj_kernel, d_f=d_f, d_sc=d_sc, d_dn=d_dn, lp=lp, l_real=l_real),
        grid=(t // tm,),
        in_specs=[row(d),
                  pl.BlockSpec((8, d), lambda i: (jnp.maximum(i * hb - 1, 0), 0)),
                  pl.BlockSpec((8, d), lambda i: (jnp.minimum((i + 1) * hb, last), 0)),
                  _resident((1, d)), _resident(wmain.shape), _resident(wlog.shape),
                  _resident(wgate.shape), _resident(wcs.shape), _resident(scw.shape), _resident(cw.shape),
                  _resident(par.shape)],
        out_specs=[row(n) for n, _ in outs],
        out_shape=[jax.ShapeDtypeStruct((t, n), dt) for n, dt in outs],
        compiler_params=_params(1),
        name="inproj",
    )(h2d, h2d, h2d, g, wmain, wlog, wgate, wcs, scw, cw, par)


def _dft_table_kernel(pc_ref, ps_ref, qc_ref, qs_ref, c_ref, s_ref):
    pc = pc_ref[...]
    ps = ps_ref[...]
    for nb in range(c_ref.shape[1] // LANE):
        qc = qc_ref[:, nb:nb + 1]
        qs = qs_ref[:, nb:nb + 1]
        c_ref[:, nb * LANE:(nb + 1) * LANE] = (qc * pc - qs * ps).astype(BF16)
        s_ref[:, nb * LANE:(nb + 1) * LANE] = (qs * pc + qc * ps).astype(BF16)


def _dft_tables(kp, l_real):
    k = jnp.arange(kp, dtype=jnp.int32) - FRONT_PAD
    valid = (k >= 0) & (k <= l_real // 2)
    kk = jnp.where(valid, k, 0)[:, None]
    theta = 2.0 * math.pi / l_real
    j = jnp.arange(LANE, dtype=jnp.int32)[None, :]
    rp = ((kk * j) % l_real).astype(F32) * theta
    ob = (jnp.arange(LANE, dtype=jnp.int32) * LANE - FRONT_PAD) % l_real
    rq = ((kk * ob[None, :]) % l_real).astype(F32) * theta
    qmask = valid[:, None] & (jnp.arange(LANE)[None, :] < kp // LANE)
    seeds = (jnp.cos(rp), jnp.sin(rp), jnp.where(qmask, jnp.cos(rq), 0.0), jnp.where(qmask, jnp.sin(rq), 0.0))
    tr = LANE
    seed = pl.BlockSpec((tr, LANE), lambda m: (m, 0))
    tab = pl.BlockSpec((tr, kp), lambda m: (m, 0))
    return pl.pallas_call(
        _dft_table_kernel,
        grid=(kp // tr,),
        in_specs=[seed] * 4,
        out_specs=[tab, tab],
        out_shape=[jax.ShapeDtypeStruct((kp, kp), BF16)] * 2,
        compiler_params=_params(1),
        name="dft_tables",
    )(*seeds)


def _dft_kernel(c_ref, s_ref, a_ref, b_ref, y_ref, ae_scr, bo_scr, yb_scr, *, scale, l_real):
    lp = a_ref.shape[1]
    kp = c_ref.shape[0]
    nb, nbk = lp // LANE, kp // LANE
    lo, hl = FRONT_PAD, l_real // 2
    q0, rem = divmod(2 * lo + l_real - (LANE - 1), LANE)
    jr = lax.broadcasted_iota(jnp.int32, (LANE, 2 * LANE), 0)
    jc = lax.broadcasted_iota(jnp.int32, (LANE, 2 * LANE), 1)
    mirror = (jr + jc == rem + LANE - 1).astype(BF16)
    rows = lax.broadcasted_iota(jnp.int32, (LANE, 1), 0)

    def mirrored(ref, n_blocks, m):
        blocks = [q0 - m, q0 - m + 1]
        if all(bi < 0 or bi >= n_blocks for bi in blocks):
            return None

        def block(bi):
            if not 0 <= bi < n_blocks:
                return jnp.zeros((LANE, ref.shape[-1]), BF16)
            sl = slice(bi * LANE, (bi + 1) * LANE)
            return ref[0, sl, :] if len(ref.shape) == 3 else ref[sl, :]

        return _dot(mirror, jnp.concatenate([block(bi) for bi in blocks], axis=0))

    def interior(rg, x):
        return jnp.where(rg > lo, jnp.where(rg < lo + hl, x, 0.0), 0.0)

    for m in range(nbk):
        rg = m * LANE + rows
        sl = slice(m * LANE, (m + 1) * LANE)
        for src, dst, sign in ((a_ref, ae_scr, 1.0), (b_ref, bo_scr, -1.0)):
            x = src[0, sl, :].astype(F32)
            xm = mirrored(src, nb, m)
            if xm is not None:
                x = x + sign * interior(rg, xm)
            dst[sl, :] = jnp.where(rg <= lo + hl, x, 0.0).astype(BF16)

    p = _dot(c_ref[...], ae_scr[...])
    q = _dot(s_ref[...], bo_scr[...])
    yb_scr[...] = interior(lax.broadcasted_iota(jnp.int32, (kp, 1), 0), (p - q) * scale).astype(BF16)
    ytop = (p + q) * scale
    for m in range(nb):
        y = ytop[m * LANE:(m + 1) * LANE] if m < nbk else None
        ym = mirrored(yb_scr, nbk, m)
        if y is None and ym is None:
            y = jnp.zeros((LANE, y_ref.shape[-1]), F32)
        elif y is None:
            y = ym
        elif ym is not None:
            y = y + ym
        y_ref[0, m * LANE:(m + 1) * LANE, :] = y.astype(BF16)


def _dft(ab, tables, l_real, d_f, scale):
    bsz, lp, _ = ab.shape
    ctab, stab = tables
    kp = ctab.shape[0]
    nw = d_f // 2
    col = lambda off: pl.BlockSpec((1, lp, nw), lambda b, j: (b, 0, off + j))
    return pl.pallas_call(
        functools.partial(_dft_kernel, scale=scale, l_real=l_real),
        grid=(bsz, 2),
        in_specs=[_resident((kp, kp)), _resident((kp, kp)), col(0), col(2)],
        out_specs=col(0),
        out_shape=jax.ShapeDtypeStruct((bsz, lp, d_f), BF16),
        scratch_shapes=[pltpu.VMEM((kp, nw), BF16)] * 3,
        compiler_params=_params(2),
        name="dft",
    )(ctab, stab, ab, ab)


def _dn_kernel(q_ref, k_ref, v_ref, z_ref, dnc_ref, r_ref, nw_ref, o_ref, acc, st, *, lp, pairs):
    c_sz, d, nh = CHUNK, DN_HEAD_DIM, DN_HEADS
    nc = lp // c_sz
    br = LANE

    ii = lax.broadcasted_iota(jnp.int32, (c_sz, 2 * c_sz), 0)
    jl = lax.broadcasted_iota(jnp.int32, (c_sz, 2 * c_sz), 1)
    left = jl < c_sz
    jj = jnp.where(left, jl, jl - c_sz)
    causal = jnp.where(left, ii - jj, jj - ii)
    tri = causal >= 0
    stri = causal > 0
    eye2 = (ii == jj).astype(F32)
    zeros_c = jnp.zeros((c_sz, 2 * d), BF16)
    zeros_d = jnp.zeros((d, d), BF16)

    def pair(x_f, x_b):
        return jnp.where(left, x_f, x_b)

    def blockdiag(x2):
        return jnp.concatenate([jnp.where(left, x2, 0.0), jnp.where(left, 0.0, x2)], axis=0).astype(BF16)

    def blockrows(x4):
        return jnp.concatenate([jnp.concatenate([x4[:, :2 * d], zeros_c], axis=1),
                                jnp.concatenate([zeros_c, x4[:, 2 * d:]], axis=1)], axis=0)

    acc[...] = jnp.zeros(acc.shape, F32)
    st[...] = jnp.zeros(st.shape, F32)

    def super_iter(it, carry):
        cfs = [it * pairs + p for p in range(pairs)]
        cbs = [nc - 1 - c for c in cfs]
        rf = [pl.multiple_of(c * c_sz, c_sz) for c in cfs]
        rb = [pl.multiple_of(c * c_sz, c_sz) for c in cbs]
        items = [(p, h) for p in range(pairs) for h in range(nh)]
        hs = lambda h: slice(h * d, (h + 1) * d)
        qrow = [(q_ref[0, pl.ds(rf[p], c_sz), :], q_ref[0, pl.ds(rb[p], c_sz), :]) for p in range(pairs)]
        krow = [(k_ref[0, pl.ds(rf[p], c_sz), :], k_ref[0, pl.ds(rb[p], c_sz), :]) for p in range(pairs)]
        vrow = [(v_ref[0, pl.ds(rf[p], c_sz), :], v_ref[0, pl.ds(rb[p], c_sz), :]) for p in range(pairs)]
        drow = [(dnc_ref[0, pl.ds(rf[p], c_sz), :], dnc_ref[0, pl.ds(rb[p], c_sz), :]) for p in range(pairs)]
        qf = [qrow[p][0][:, hs(h)] for p, h in items]
        qb = [qrow[p][1][:, hs(h)] for p, h in items]
        kf = [krow[p][0][:, hs(h)] for p, h in items]
        kb = [krow[p][1][:, hs(h)] for p, h in items]
        vf = [vrow[p][0][:, hs(h)].astype(F32) for p, h in items]
        vb = [vrow[p][1][:, hs(h)].astype(F32) for p, h in items]
        btf = [drow[p][0][:, h:h + 1] for p, h in items]
        btb = [drow[p][1][:, nh + h:nh + h + 1] for p, h in items]
        gcf = [drow[p][0][:, 2 * nh + h:2 * nh + h + 1] for p, h in items]
        gcb = [drow[p][1][:, 3 * nh + h:3 * nh + h + 1] for p, h in items]
        grow = [pair(r_ref[0, h, pl.ds(cfs[p], 1), :], r_ref[0, h, pl.ds(cbs[p], 1), :]) for p, h in items]
        dec = [jnp.where(tri, jnp.exp(jnp.minimum(pair(f, b) - g, 0.0)), 0.0)
               for f, b, g in zip(gcf, gcb, grow)]
        gram = [lax.dot_general(
            jnp.concatenate([jnp.concatenate([a, b], axis=1), jnp.concatenate([c, e], axis=1)], axis=0),
            jnp.concatenate([jnp.concatenate([c, zeros_c[:, :d]], axis=1),
                             jnp.concatenate([zeros_c[:, :d], e], axis=1)], axis=0),
            _NT, preferred_element_type=F32) for a, b, c, e in zip(qf, qb, kf, kb)]
        nm = [jnp.where(stri, -(pair(f, b) * g[c_sz:] * dc), 0.0)
              for f, b, g, dc in zip(btf, btb, gram, dec)]
        sacc = [eye2 + x for x in nm]
        qpow = [_dot(x.astype(BF16), blockdiag(x)) for x in nm]
        for _ in range(4):
            r = [_dot(jnp.concatenate([s, q], axis=0).astype(BF16), blockdiag(q)) for s, q in zip(sacc, qpow)]
            sacc = [s + x[:c_sz] for s, x in zip(sacc, r)]
            qpow = [x[c_sz:] for x in r]
        tinv = [s + _dot(s.astype(BF16), blockdiag(q)) for s, q in zip(sacc, qpow)]
        egf = [jnp.exp(x) for x in gcf]
        egb = [jnp.exp(x) for x in gcb]
        kf32 = [x.astype(F32) for x in kf]
        kb32 = [x.astype(F32) for x in kb]
        rhs = [blockrows(jnp.concatenate([v1 * bf, k1 * (bf * ef), v2 * bb, k2 * (bb * eb)], axis=1).astype(BF16))
               for v1, k1, v2, k2, bf, bb, ef, eb in zip(vf, kf32, vb, kb32, btf, btb, egf, egb)]
        uw = [blockrows(_dot(t.astype(BF16), x).astype(BF16)) for t, x in zip(tinv, rhs)]
        x1 = [_dot((g[:c_sz] * dc).astype(BF16), x) for g, dc, x in zip(gram, dec, uw)]
        gtf = [x[c_sz - 1:c_sz] for x in gcf]
        gtb = [x[0:1] for x in gcb]
        kd = [jnp.concatenate([k1 * jnp.exp(tf - f), k2 * jnp.exp(tb - b)], axis=0).astype(BF16)
              for k1, k2, tf, tb, f, b in zip(kf32, kb32, gtf, gtb, gcf, gcb)]
        x2 = [lax.dot_general(a, x, _TN, preferred_element_type=F32) for a, x in zip(kd, uw)]
        sel = lambda x, o: jnp.concatenate([x[:, o:o + d], x[:, 2 * d + o:3 * d + o]], axis=1)
        lhs = [jnp.concatenate([
            (jnp.concatenate([a.astype(F32) * ef, b.astype(F32) * eb], axis=1) - sel(y1, d)).astype(BF16),
            (-sel(y2, d)).astype(BF16)], axis=0)
            for a, b, ef, eb, y1, y2 in zip(qf, qb, egf, egb, x1, x2)]
        bo = [sel(y1, 0) for y1 in x1]
        bs = [sel(y2, 0) for y2 in x2]
        gl = [jnp.concatenate([jnp.broadcast_to(jnp.exp(tf), (1, d)), jnp.broadcast_to(jnp.exp(tb), (1, d))],
                              axis=1) for tf, tb in zip(gtf, gtb)]

        state = [st[h] for h in range(nh)]
        for n, (p, h) in enumerate(items):
            s2 = state[h]
            sbd = jnp.concatenate([jnp.concatenate([s2[:, :d].astype(BF16), zeros_d], axis=1),
                                   jnp.concatenate([zeros_d, s2[:, d:].astype(BF16)], axis=1)], axis=0)
            r = _dot(lhs[n], sbd)
            o2 = r[:c_sz] + bo[n]
            state[h] = gl[n] * s2 + r[c_sz:] + bs[n]
            acc[pl.ds(rf[p], c_sz), hs(h)] += o2[:, :d]
            acc[pl.ds(rb[p], c_sz), hs(h)] += o2[:, d:]
        for h in range(nh):
            st[h] = state[h]
        return carry

    lax.fori_loop(0, nc // pairs, super_iter, 0)

    def fin(i, carry):
        s = pl.multiple_of(i * br, br)
        zz = z_ref[0, pl.ds(s, br), :].astype(F32)
        for h in range(nh):
            o = acc[pl.ds(s, br), h * d:(h + 1) * d]
            o = o * lax.rsqrt(jnp.mean(o * o, axis=-1, keepdims=True) + EPS) * nw_ref[...]
            o_ref[0, pl.ds(s, br), h * d:(h + 1) * d] = (o * _silu(zz[:, h * d:(h + 1) * d])).astype(BF16)
        return carry

    lax.fori_loop(0, lp // br, fin, 0)


def _deltanet(qkv, z, dnc, rrow, norm_w):
    bsz, lp, _ = qkv.shape
    d, nh = DN_HEAD_DIM, DN_HEADS
    nc = lp // CHUNK
    pairs = next(g for g in (3, 2, 1) if nc % g == 0)
    once = lambda blk, idx: pl.BlockSpec(blk, idx, pipeline_mode=pl.Buffered(1))
    return pl.pallas_call(
        functools.partial(_dn_kernel, lp=lp, pairs=pairs),
        grid=(bsz,),
        in_specs=[once((1, lp, nh * d), lambda b: (b, 0, 0)),
                  once((1, lp, nh * d), lambda b: (b, 0, 1)),
                  once((1, lp, nh * d), lambda b: (b, 0, 2)),
                  once((1, lp, nh * d), lambda b: (b, 0, 0)),
                  pl.BlockSpec((1, lp, LANE), lambda b: (b, 0, 0)),
                  pl.BlockSpec((1, nh, nc, 2 * CHUNK), lambda b: (b, 0, 0, 0)),
                  _resident((1, d))],
        out_specs=pl.BlockSpec((1, lp, nh * d), lambda b: (b, 0, 0)),
        out_shape=jax.ShapeDtypeStruct((bsz, lp, nh * d), BF16),
        scratch_shapes=[pltpu.VMEM((lp, nh * d), F32), pltpu.VMEM((nh, d, 2 * d), F32)],
        compiler_params=_params(1),
        name="deltanet",
    )(qkv, qkv, qkv, z, dnc, rrow, norm_w)


def _merge_kernel(h_ref, y_ref, sc_ref, dn_ref, gate_ref, wf_ref, wsc_ref, wdn_ref, wout_ref, gpost_ref,
                  o_ref, *, d_model):
    y_f = _dot(y_ref[...], wf_ref[...])
    y_sc = _dot(sc_ref[...], wsc_ref[...])
    y_dn = _dot(dn_ref[...], wdn_ref[...])
    merged = (gate_ref[:, 0:d_model].astype(F32) * y_f
              + gate_ref[:, d_model:2 * d_model].astype(F32) * y_sc
              + gate_ref[:, 2 * d_model:3 * d_model].astype(F32) * y_dn)
    out = _dot(merged.astype(BF16), wout_ref[...])
    o_ref[...] = h_ref[...] + _rms(out, gpost_ref[...])


def _merge(h2d, y, sc, dn, gate, wf, wsc, wdn, wout, gpost, tm):
    t, d = h2d.shape
    row = lambda n: pl.BlockSpec((tm, n), lambda i: (i, 0))
    return pl.pallas_call(
        functools.partial(_merge_kernel, d_model=d),
        grid=(t // tm,),
        in_specs=[row(d), row(y.shape[1]), row(sc.shape[1]), row(dn.shape[1]), row(gate.shape[1]),
                  _resident(wf.shape), _resident(wsc.shape), _resident(wdn.shape),
                  _resident(wout.shape), _resident((1, d))],
        out_specs=row(d),
        out_shape=jax.ShapeDtypeStruct((t, d), F32),
        compiler_params=_params(1),
        name="merge",
    )(h2d, y, sc, dn, gate, wf, wsc, wdn, wout, gpost)


def _channel_dft_matrix(d_f):
    gd = FOURIER_GROUP_DIM
    idx = np.arange(gd)
    ang = 2.0 * np.pi * ((idx[:, None] * idx[None, :]) % gd) / gd
    blk = np.eye(d_f // gd)
    return np.concatenate([np.kron(blk, np.cos(ang)), np.kron(blk, -np.sin(ang))], axis=1)


def _pick_tile(n, candidates):
    return next(c for c in candidates if n % c == 0)


def kernel(x, meta_tokens, norm_gains, ffn1_wi, ffn1_wo, ffn2_wi, ffn2_wo, w_in, sconv_w, dn_conv_w,
           dn_A_log, dn_dt_bias, dn_norm, w_fourier, w_sconv_out, w_dn_out, w_out):
    bsz, seq, d = x.shape
    depth = norm_gains.shape[0]
    d_f = w_fourier.shape[1]
    d_sc = w_sconv_out.shape[1]
    d_dn = w_dn_out.shape[1]
    nh = DN_HEADS
    n_log = 4 * nh
    l_real = N_META + seq
    lp = -(-(FRONT_PAD + l_real) // LANE) * LANE
    nc = lp // CHUNK
    t = bsz * lp
    tm = _pick_tile(t, (1024, 768, 512, 384, 256, 128))
    tmi = _pick_tile(t, (1024, 512, 384, 256, 128))
    assert l_real % 2 == 0, "the half-spectrum DFT needs an even number of positions"
    kp = -(-(FRONT_PAD + l_real // 2 + 1) // LANE) * LANE

    tmx = _pick_tile(seq, (1024, 512, 256, 128, 64))
    seq_rows = (bsz, lp, FRONT_PAD + N_META, seq)
    head_in = jnp.concatenate([jnp.zeros((FRONT_PAD, d), x.dtype), meta_tokens.astype(x.dtype)], axis=0)

    wcs = jnp.asarray(_channel_dft_matrix(d_f), F32).astype(BF16)
    tables = _dft_tables(kp, l_real)
    scale = 1.0 / math.sqrt(l_real * FOURIER_GROUP_DIM)
    o_main = d_f + 3 * d_sc + 4 * d_dn

    for l in range(depth):
        gains = norm_gains[l][:, None, :]
        wi1, wo1 = ffn1_wi[l].astype(BF16), ffn1_wo[l].astype(BF16)
        if l == 0:
            h = _ffn(x.reshape(bsz * seq, d), gains[0], gains[1], wi1, wo1, tmx, pad=seq_rows)
            head = _ffn(head_in, gains[0], gains[1], wi1, wo1, CHUNK)
            h = _fill_edges(h, head, bsz, lp, lp - FRONT_PAD - l_real)
        else:
            h = _ffn(h, gains[0], gains[1], wi1, wo1, tm)

        w = w_in[l]
        wlog = jnp.pad(w[:, o_main:o_main + n_log], ((0, 0), (0, LANE - n_log))).astype(BF16)
        par = jnp.zeros((8, LANE), F32)
        par = par.at[0, 2 * nh:n_log].set(dn_A_log[l].reshape(-1))
        par = par.at[1, 2 * nh:n_log].set(dn_dt_bias[l].reshape(-1))
        ab, sc, qkv, z, dnc, gate = _inproj(
            h, gains[2], w[:, :o_main].astype(BF16), wlog, w[:, o_main + n_log:].astype(BF16), wcs,
            sconv_w[l], dn_conv_w[l], par, tmi, d_f, d_sc, d_dn, lp, l_real)
        y = _dft(ab.reshape(bsz, lp, 2 * d_f), tables, l_real, d_f, scale)
        r = jnp.transpose(dnc.reshape(bsz, nc, CHUNK, LANE)[..., 2 * nh:n_log], (0, 3, 1, 2))
        rrow = jnp.concatenate([r[:, :nh], r[:, nh:]], axis=-1)
        dn = _deltanet(qkv.reshape(bsz, lp, 3 * d_dn), z.reshape(bsz, lp, d_dn),
                       dnc.reshape(bsz, lp, LANE), rrow, dn_norm[l][None, :])
        h = _merge(h, y.reshape(t, d_f), sc, dn.reshape(t, d_dn), gate,
                   w_fourier[l].astype(BF16), w_sconv_out[l].astype(BF16), w_dn_out[l].astype(BF16),
                   w_out[l].astype(BF16), gains[3], tm)

        if l + 1 < depth:
            h = _ffn(h, gains[4], gains[5], ffn2_wi[l].astype(BF16), ffn2_wo[l].astype(BF16), tm)

    out = _ffn(h, gains[4], gains[5], ffn2_wi[depth - 1].astype(BF16), ffn2_wo[depth - 1].astype(BF16), tmx,
               unpad=seq_rows)
    return out.reshape(bsz, seq, d)
```
